```python
import math
import jax, jax.numpy as jnp
from jax import lax
import numpy as np

D_MODEL = 1024
BATCH = 4
SEQ = 8192
DEPTH = 1
DEC_BATCH = 32
DEC_SEQ = 1
PAST_LEN = 16384
PAGE_SIZE = 128

MIX_WIDTH = D_MODEL
H_GLA = 4
DV_GLA = MIX_WIDTH // 2 // H_GLA
DK_GLA = DV_GLA // 2
GK_RANK = 16
GK_NORMALIZER = 16.0
GLA_CHUNK = 64
H_DIFF = 4
DV_DIFF = MIX_WIDTH // 2 // H_DIFF
DH_DIFF = DV_DIFF // 2
Q_BLOCK = 128
D_FF = 4 * D_MODEL
EPS = 1e-6
GLA_QK_W = H_GLA * DK_GLA
GLA_V_W = H_GLA * DV_GLA
DIFF_QK_W = H_DIFF * 2 * DH_DIFF
DIFF_V_W = H_DIFF * DV_DIFF
D_IN = 2 * GLA_QK_W + 2 * GLA_V_W + GK_RANK + 2 * DIFF_QK_W + DIFF_V_W

kernel_name = "hybrid_gla_diffattn_decode_step"


def _split_points():
    sizes = [GLA_QK_W, GLA_QK_W, GLA_V_W, GLA_V_W, GK_RANK, DIFF_QK_W, DIFF_QK_W, DIFF_V_W]
    return [int(s) for s in np.cumsum(sizes)[:-1]]


def rmsnorm(x, g):
    xf = x.astype(jnp.float32)
    r = lax.rsqrt(jnp.mean(xf * xf, axis=-1, keepdims=True) + EPS)
    return (xf * r).astype(x.dtype) * g


def gla_block(S, q, k, v, log_a):
    f32 = jnp.float32
    q = q.astype(f32); k = k.astype(f32); v = v.astype(f32)
    b = jnp.cumsum(log_a.astype(f32), axis=1)
    C = q.shape[1]
    causal = jnp.tril(jnp.ones((C, C), dtype=bool))
    expo = b[:, :, None] - b[:, None, :]
    decay = jnp.exp(jnp.where(causal[None, :, :, None, None], expo, -jnp.inf))
    attn = jnp.einsum('bthk,bshk,btshk->bhts', q, k, decay)
    o_intra = jnp.einsum('bhts,bshv->bthv', attn, v)
    o_inter = jnp.einsum('bthk,bhkv->bthv', q * jnp.exp(b), S)
    b_last = b[:, -1]
    k_dec = k * jnp.exp(b_last[:, None] - b)
    S_new = S * jnp.exp(b_last)[..., None] + jnp.einsum('bshk,bshv->bhkv', k_dec, v)
    return S_new, o_intra + o_inter


def gla_prompt(q, k, v, log_a):
    B, T = q.shape[:2]
    n = T // GLA_CHUNK

    def to_blocks(a):
        return a.reshape(B, n, GLA_CHUNK, *a.shape[2:]).swapaxes(0, 1)

    S0 = jnp.zeros((B, H_GLA, DK_GLA, DV_GLA), jnp.float32)
    S, o = lax.scan(lambda s, xs: gla_block(s, *xs), S0,
                    (to_blocks(q), to_blocks(k), to_blocks(v), to_blocks(log_a)))
    return S, o.swapaxes(0, 1).reshape(B, T, H_GLA, DV_GLA)


def diff_attn_prompt(q, k, v, lam):
    B, T = q.shape[:2]
    nb = T // Q_BLOCK
    scale = DH_DIFF ** -0.5
    qb = q.reshape(B, nb, Q_BLOCK, H_DIFF, 2, DH_DIFF).swapaxes(0, 1)
    kpos = jnp.arange(T)

    def one_block(args):
        i, qi = args
        s = jnp.einsum('bqhcd,bkhcd->bhcqk', qi, k).astype(jnp.float32) * scale
        qpos = i * Q_BLOCK + jnp.arange(Q_BLOCK)
        mask = kpos[None, :] <= qpos[:, None]
        p = jax.nn.softmax(jnp.where(mask, s, -jnp.inf), axis=-1)
        a = p[:, :, 0] - lam * p[:, :, 1]
        return jnp.einsum('bhqk,bkhv->bqhv', a.astype(v.dtype), v)

    o = lax.map(one_block, (jnp.arange(nb), qb))
    return o.swapaxes(0, 1).reshape(B, T, H_DIFF, DV_DIFF)


def diff_attn_sample(q, k_new, v_new, k_past, v_past, lam):
    Tn = q.shape[1]
    P = k_past.shape[1]
    scale = DH_DIFF ** -0.5
    s_past = jnp.einsum('bqhcd,bkhcd->bhcqk', q, k_past).astype(jnp.float32) * scale
    s_new = jnp.einsum('bqhcd,bkhcd->bhcqk', q, k_new).astype(jnp.float32) * scale
    causal = jnp.tril(jnp.ones((Tn, Tn), dtype=bool))
    s = jnp.concatenate([s_past, jnp.where(causal, s_new, -jnp.inf)], axis=-1)
    p = jax.nn.softmax(s, axis=-1)
    a = (p[:, :, 0] - lam * p[:, :, 1]).astype(v_new.dtype)
    return (jnp.einsum('bhqk,bkhv->bqhv', a[..., :P], v_past)
            + jnp.einsum('bhqk,bkhv->bqhv', a[..., P:], v_new))


def hybrid_layer(x, p, gla_fn, attn_fn, lambda_init):
    B, T, _ = x.shape
    h = rmsnorm(x, p['norm_mix_g'])
    z = jnp.einsum('btd,de->bte', h, p['w_in'])
    gq, gk, gv, gg, glr, dq, dk, dv = jnp.split(z, _split_points(), axis=-1)
    q_g = gq.reshape(B, T, H_GLA, DK_GLA) * (DK_GLA ** -0.5)
    k_g = gk.reshape(B, T, H_GLA, DK_GLA)
    v_g = gv.reshape(B, T, H_GLA, DV_GLA)
    gate_logit = jnp.einsum('btr,rk->btk', glr, p['w_gk2']) + p['b_gk']
    log_a = (jax.nn.log_sigmoid(gate_logit.astype(jnp.float32)) / GK_NORMALIZER).reshape(B, T, H_GLA, DK_GLA)
    S_new, o_g = gla_fn(q_g, k_g, v_g, log_a)
    o_g = rmsnorm(o_g.astype(x.dtype), p['gla_norm_g']) * jax.nn.silu(gg.reshape(B, T, H_GLA, DV_GLA))
    q_d = rmsnorm(dq.reshape(B, T, H_DIFF, 2, DH_DIFF), p['q_norm_g'])
    k_d = rmsnorm(dk.reshape(B, T, H_DIFF, 2, DH_DIFF), p['k_norm_g'])
    v_d = dv.reshape(B, T, H_DIFF, DV_DIFF)
    lam = (jnp.exp(jnp.sum((p['lambda_q1'] * p['lambda_k1']).astype(jnp.float32)))
           - jnp.exp(jnp.sum((p['lambda_q2'] * p['lambda_k2']).astype(jnp.float32)))
           + lambda_init)
    o_d = attn_fn(q_d, k_d, v_d, lam)
    o_d = rmsnorm(o_d, p['diff_norm_g']) * (1.0 - lambda_init)
    mix = jnp.concatenate([o_g.reshape(B, T, -1), o_d.reshape(B, T, -1).astype(o_g.dtype)], axis=-1)
    x = x + jnp.einsum('bte,ed->btd', mix, p['w_out'])
    h2 = rmsnorm(x, p['norm_ffn_g'])
    u = jnp.square(jax.nn.relu(jnp.einsum('btd,df->btf', h2, p['w_up'])))
    x = x + jnp.einsum('btf,fd->btd', u, p['w_down'])
    return x, k_d, v_d, S_new


def setup_inputs(seed: int = 0) -> dict:
    key = jax.random.key(seed)
    ks = jax.random.split(key, 24)
    f32 = jnp.float32
    n_pages = PAST_LEN // PAGE_SIZE
    n_pool = (DEC_BATCH * n_pages * 5) // 4
    nrm = lambda k, shape, s: jax.random.normal(k, shape, f32) * s
    gain = lambda k, n: 1.0 + 0.02 * jax.random.normal(k, (DEPTH, n), f32)
    page_table = jax.random.permutation(ks[5], n_pool)[: DEC_BATCH * n_pages]
    page_table = page_table.reshape(DEC_BATCH, n_pages).astype(jnp.int32)
    return {
        'x_prompt': nrm(ks[0], (BATCH, SEQ, D_MODEL), 1.0),
        'x_sample': nrm(ks[1], (DEC_BATCH, DEC_SEQ, D_MODEL), 1.0),
        'cache_k': nrm(ks[2], (DEPTH, n_pool, PAGE_SIZE, H_DIFF, 2, DH_DIFF), 1.0),
        'cache_v': nrm(ks[3], (DEPTH, n_pool, PAGE_SIZE, H_DIFF, DV_DIFF), 1.0),
        'page_table': page_table,
        'state_gla': nrm(ks[4], (DEPTH, DEC_BATCH, H_GLA, DK_GLA, DV_GLA), 0.5),
        'norm_mix_g': gain(ks[6], D_MODEL),
        'w_in': nrm(ks[7], (DEPTH, D_MODEL, D_IN), D_MODEL ** -0.5),
        'w_gk2': nrm(ks[8], (DEPTH, GK_RANK, GLA_QK_W), GK_RANK ** -0.5),
        'b_gk': nrm(ks[9], (DEPTH, GLA_QK_W), 0.01),
        'gla_norm_g': gain(ks[10], DV_GLA),
        'q_norm_g': gain(ks[11], DH_DIFF),
        'k_norm_g': gain(ks[12], DH_DIFF),
        'lambda_q1': nrm(ks[13], (DEPTH, DH_DIFF), 0.1),
        'lambda_k1': nrm(ks[14], (DEPTH, DH_DIFF), 0.1),
        'lambda_q2': nrm(ks[15], (DEPTH, DH_DIFF), 0.1),
        'lambda_k2': nrm(ks[16], (DEPTH, DH_DIFF), 0.1),
        'diff_norm_g': gain(ks[17], DV_DIFF),
        'w_out': nrm(ks[18], (DEPTH, MIX_WIDTH, D_MODEL), MIX_WIDTH ** -0.5),
        'norm_ffn_g': gain(ks[19], D_MODEL),
        'w_up': nrm(ks[20], (DEPTH, D_MODEL, D_FF), D_MODEL ** -0.5),
        'w_down': nrm(ks[21], (DEPTH, D_FF, D_MODEL), D_FF ** -0.5),
    }


def reference(x_prompt, x_sample, cache_k, cache_v, page_table, state_gla,
              norm_mix_g, w_in, w_gk2, b_gk, gla_norm_g, q_norm_g, k_norm_g,
              lambda_q1, lambda_k1, lambda_q2, lambda_k2, diff_norm_g, w_out,
              norm_ffn_g, w_up, w_down):
    Bd = x_sample.shape[0]
    past = page_table.shape[1] * PAGE_SIZE
    yp, ys = x_prompt, x_sample
    kp_l, vp_l, sp_l, ks_l, vs_l, ss_l = [], [], [], [], [], []
    for l in range(DEPTH):
        lambda_init = 0.8 - 0.6 * math.exp(-0.3 * l)
        p = {
            'norm_mix_g': norm_mix_g[l], 'w_in': w_in[l], 'w_gk2': w_gk2[l], 'b_gk': b_gk[l],
            'gla_norm_g': gla_norm_g[l], 'q_norm_g': q_norm_g[l], 'k_norm_g': k_norm_g[l],
            'lambda_q1': lambda_q1[l], 'lambda_k1': lambda_k1[l],
            'lambda_q2': lambda_q2[l], 'lambda_k2': lambda_k2[l],
            'diff_norm_g': diff_norm_g[l], 'w_out': w_out[l], 'norm_ffn_g': norm_ffn_g[l],
            'w_up': w_up[l], 'w_down': w_down[l],
        }
        yp, kp, vp, sp = hybrid_layer(yp, p, gla_prompt, diff_attn_prompt, lambda_init)
        k_past = cache_k[l, page_table].reshape(Bd, past, H_DIFF, 2, DH_DIFF)
        v_past = cache_v[l, page_table].reshape(Bd, past, H_DIFF, DV_DIFF)
        s0 = state_gla[l].astype(jnp.float32)
        gla_fn = lambda q, k, v, a, s0=s0: gla_block(s0, q, k, v, a)
        attn_fn = lambda q, k, v, lam, kp_=k_past, vp_=v_past: diff_attn_sample(q, k, v, kp_, vp_, lam)
        ys, ksm, vsm, ssm = hybrid_layer(ys, p, gla_fn, attn_fn, lambda_init)
        kp_l.append(kp); vp_l.append(vp); sp_l.append(sp)
        ks_l.append(ksm); vs_l.append(vsm); ss_l.append(ssm)
    k_prompt = jnp.stack(kp_l); v_prompt = jnp.stack(vp_l); gla_state_prompt = jnp.stack(sp_l)
    k_sample = jnp.stack(ks_l); v_sample = jnp.stack(vs_l); gla_state_sample = jnp.stack(ss_l)
    return (yp, ys, k_prompt, v_prompt, gla_state_prompt, k_sample, v_sample, gla_state_sample)
```

```python
import functools
import math

import jax
import jax.numpy as jnp
import numpy as np
from jax import lax
from jax.experimental import pallas as pl
from jax.experimental.pallas import tpu as pltpu

F32 = jnp.float32
BF16 = jnp.bfloat16

LANES = 128
SUBLANES = 8
VMEM_LIMIT = 56 * 1024 * 1024

D_MODEL = 1024
H_GLA = 4
DK_GLA = 64
DV_GLA = 128
GK_RANK = 16
GK_NORMALIZER = 16.0
H_DIFF = 4
DH_DIFF = 64
DV_DIFF = 128
D_FF = 4 * D_MODEL
EPS = 1e-6
LAMBDA_INIT = 0.8 - 0.6 * math.exp(-0.3 * 0)

GLA_QK_W = H_GLA * DK_GLA
GLA_V_W = H_GLA * DV_GLA
DIFF_QK_W = H_DIFF * 2 * DH_DIFF
DIFF_V_W = H_DIFF * DV_DIFF
OFF_GLR = 2 * GLA_QK_W + 2 * GLA_V_W
OFF_DIFF = OFF_GLR + GK_RANK

LOG2E = 1.4426950408889634
NEG_BIG = -1e30

GLA_C = 256
GLA_DIAG = 8
ATT_T = 256
DEC_PAGES = 8


def _cparams(sem):
    return pltpu.CompilerParams(dimension_semantics=sem, vmem_limit_bytes=VMEM_LIMIT)


def _const_spec(shape):
    nd = len(shape)
    return pl.BlockSpec(shape, lambda *_: (0,) * nd)


def _dot(a, b):
    return jnp.dot(a, b, preferred_element_type=F32)


def _dot_nt(a, b):
    return lax.dot_general(a, b, (((1,), (1,)), ((), ())), preferred_element_type=F32)


def _lambda_full(lq1, lk1, lq2, lk2):
    a = jnp.sum(lq1 * lk1, axis=-1, keepdims=True)
    b = jnp.sum(lq2 * lk2, axis=-1, keepdims=True)
    return jnp.exp(a) - jnp.exp(b) + LAMBDA_INIT


def _group_rms(z, ones_blk, group):
    outs = []
    w = ones_blk.shape[0]
    for c in range(z.shape[1] // w):
        zc = z[:, c * w:(c + 1) * w]
        outs.append(_dot((zc * zc).astype(BF16), ones_blk))
    ss = jnp.concatenate(outs, axis=1)
    return lax.rsqrt(ss * (1.0 / group) + EPS)


def _in_proj_kernel(x_ref, g_ref, wgla_ref, wglr_ref, wgk2_ref, bgk_ref, wdiff_ref,
                    qng_ref, kng_ref, ones_ref,
                    qg_ref, kg_ref, vg_ref, gg_ref, la_ref,
                    qd_ref, kdf_ref, kdb_ref, vdf_ref, vdb_ref):
    x = x_ref[...]
    r = lax.rsqrt(jnp.mean(x * x, axis=-1, keepdims=True) + EPS)
    h = ((x * r) * g_ref[...]).astype(BF16)

    qg_ref[...] = _dot(h, wgla_ref[:, 0:GLA_QK_W]) * (DK_GLA ** -0.5)
    kg_ref[...] = _dot(h, wgla_ref[:, GLA_QK_W:2 * GLA_QK_W])
    vg_ref[...] = _dot(h, wgla_ref[:, 2 * GLA_QK_W:2 * GLA_QK_W + GLA_V_W])
    gg_ref[...] = _dot(h, wgla_ref[:, 2 * GLA_QK_W + GLA_V_W:])

    glr = _dot(h, wglr_ref[...])
    gl = _dot(glr.astype(BF16), wgk2_ref[...]) + bgk_ref[...]
    log_sig = jnp.minimum(gl, 0.0) - jnp.log1p(jnp.exp(-jnp.abs(gl)))
    la_ref[...] = log_sig / GK_NORMALIZER

    ones_blk = ones_ref[...]
    dq = _dot(h, wdiff_ref[:, 0:DIFF_QK_W])
    qn = (dq * _group_rms(dq, ones_blk, DH_DIFF)) * qng_ref[...]
    qd_ref[...] = (qn * (DH_DIFF ** -0.5 * LOG2E)).astype(BF16)
    dk = _dot(h, wdiff_ref[:, DIFF_QK_W:2 * DIFF_QK_W])
    kn = (dk * _group_rms(dk, ones_blk, DH_DIFF)) * kng_ref[...]
    kdf_ref[...] = kn
    kdb_ref[...] = kn.astype(BF16)
    dv = _dot(h, wdiff_ref[:, 2 * DIFF_QK_W:])
    vdf_ref[...] = dv
    vdb_ref[...] = dv.astype(BF16)


def _in_proj(x, wts, tm):
    n = x.shape[0]
    assert n % tm == 0
    row = lambda w: pl.BlockSpec((tm, w), lambda i: (i, 0))
    out_w = [(GLA_QK_W, F32), (GLA_QK_W, F32), (GLA_V_W, F32), (GLA_V_W, F32), (GLA_QK_W, F32),
             (DIFF_QK_W, BF16), (DIFF_QK_W, F32), (DIFF_QK_W, BF16), (DIFF_V_W, F32), (DIFF_V_W, BF16)]
    consts = [wts['g_mix'], wts['w_gla'], wts['w_glr'], wts['w_gk2'], wts['b_gk'], wts['w_diff'],
              wts['qn_g'], wts['kn_g'], wts['ones_blk']]
    return pl.pallas_call(
        _in_proj_kernel,
        grid=(n // tm,),
        in_specs=[row(D_MODEL)] + [_const_spec(c.shape) for c in consts],
        out_specs=[row(w) for w, _ in out_w],
        out_shape=[jax.ShapeDtypeStruct((n, w), dt) for w, dt in out_w],
        compiler_params=_cparams(("parallel",)),
        name="in_proj",
    )(x, *consts)


def _bcast_row(x3, j):
    return jnp.broadcast_to(x3[:, j:j + 1, :], x3.shape)


def _gla_prompt_kernel(q_ref, k_ref, la_ref, v_ref, gate_ref, tri_ref, ind_ref, gn_ref,
                       o_ref, s_ref, st_ref):
    c = pl.program_id(2)
    C = q_ref.shape[0]
    neg_inf = -jnp.inf

    @pl.when(c == 0)
    def _():
        st_ref[...] = jnp.zeros_like(st_ref)

    la = la_ref[...]
    tri = tri_ref[...]
    t1 = la.astype(BF16)
    r1 = la - t1.astype(F32)
    t2 = r1.astype(BF16)
    t3 = (r1 - t2.astype(F32)).astype(BF16)
    b = _dot(tri, t1) + _dot(tri, t2) + _dot(tri, t3)

    q = q_ref[...]
    k = k_ref[...]
    v = v_ref[...]
    vb = v.astype(BF16)
    lane = lax.broadcasted_iota(jnp.int32, (C, LANES), 1)
    head0 = lane < DK_GLA

    G = C // GLA_DIAG
    b3 = b.reshape(G, GLA_DIAG, LANES)
    q3 = q.reshape(G, GLA_DIAG, LANES)
    k3 = k.reshape(G, GLA_DIAG, LANES)
    v3 = v.reshape(G, GLA_DIAG, 2 * DV_GLA)
    sub = lax.broadcasted_iota(jnp.int32, (G, GLA_DIAG, LANES), 1)
    ind = ind_ref[...]
    o = jnp.zeros((C, 2 * DV_GLA), F32)
    for j in range(GLA_DIAG):
        e = jnp.exp(jnp.where(sub >= j, b3 - _bcast_row(b3, j), neg_inf))
        pj = (q3 * _bcast_row(k3, j) * e).reshape(C, LANES).astype(BF16)
        rj = _dot(pj, ind)
        o = o + rj * _bcast_row(v3, j).reshape(C, 2 * DV_GLA)

    ts_xor = (lax.broadcasted_iota(jnp.int32, (C, C), 0)
              ^ lax.broadcasted_iota(jnp.int32, (C, C), 1))
    tok = lax.broadcasted_iota(jnp.int32, (C, LANES), 0)
    a0 = jnp.zeros((C, C), F32)
    a1 = jnp.zeros((C, C), F32)
    m = GLA_DIAG
    while m < C:
        P = C // (2 * m)
        bmid = _bcast_row(b.reshape(P, 2 * m, LANES), m).reshape(C, LANES)
        late = (tok & m) != 0
        qt = q * jnp.exp(jnp.where(late, b - bmid, neg_inf))
        kt = (k * jnp.exp(jnp.where(late, neg_inf, bmid - b))).astype(BF16)
        split_here = (ts_xor // m) == 1
        x0 = _dot_nt(jnp.where(head0, qt, 0.0).astype(BF16), kt)
        x1 = _dot_nt(jnp.where(head0, 0.0, qt).astype(BF16), kt)
        a0 = jnp.where(split_here, x0, a0)
        a1 = jnp.where(split_here, x1, a1)
        m *= 2
    o_intra = jnp.concatenate([_dot(a0.astype(BF16), vb[:, :DV_GLA]),
                               _dot(a1.astype(BF16), vb[:, DV_GLA:])], axis=1)

    st = st_ref[...]
    stb = st.astype(BF16)
    qe = q * jnp.exp(b)
    o_inter = jnp.concatenate(
        [_dot_nt(jnp.where(head0, qe, 0.0).astype(BF16), stb[:DV_GLA]),
         _dot_nt(jnp.where(head0, 0.0, qe).astype(BF16), stb[DV_GLA:])], axis=1)
    o = o + o_intra + o_inter

    b_last = b[C - 1:C, :]
    kd = (k * jnp.exp(b_last - b)).astype(BF16)
    kv0 = _dot(vb[:, :DV_GLA].T, kd)
    kv1 = _dot(vb[:, DV_GLA:].T, kd)
    head0_s = lax.broadcasted_iota(jnp.int32, (DV_GLA, LANES), 1) < DK_GLA
    dec = jnp.exp(b_last)
    st_new = jnp.concatenate([st[:DV_GLA] * dec + jnp.where(head0_s, kv0, 0.0),
                              st[DV_GLA:] * dec + jnp.where(head0_s, 0.0, kv1)], axis=0)
    st_ref[...] = st_new

    gate = gate_ref[...]
    gn = gn_ref[...]
    outs = []
    for hh in range(2):
        oh = o[:, hh * DV_GLA:(hh + 1) * DV_GLA]
        gh = gate[:, hh * DV_GLA:(hh + 1) * DV_GLA]
        r = lax.rsqrt(jnp.mean(oh * oh, axis=-1, keepdims=True) + EPS)
        outs.append(((oh * r) * gn) * (gh * (1.0 / (1.0 + jnp.exp(-gh)))))
    o_ref[...] = jnp.concatenate(outs, axis=1).astype(o_ref.dtype)

    @pl.when(c == pl.num_programs(2) - 1)
    def _():
        s0 = st_new[:DV_GLA].T
        s1 = st_new[DV_GLA:].T
        s_ref[0, 0] = s0[:DK_GLA]
        s_ref[0, 1] = s1[DK_GLA:]


def _gla_prompt(qg, kg, la, vg, gg, wts, batch, seq):
    C = GLA_C
    assert seq % C == 0
    nc = seq // C
    npair = H_GLA // 2
    qk_spec = pl.BlockSpec((C, LANES), lambda b, p, c: (b * nc + c, p))
    v_spec = pl.BlockSpec((C, 2 * DV_GLA), lambda b, p, c: (b * nc + c, p))
    return pl.pallas_call(
        _gla_prompt_kernel,
        grid=(batch, npair, nc),
        in_specs=[qk_spec, qk_spec, qk_spec, v_spec, v_spec,
                  _const_spec((C, C)), _const_spec((LANES, 2 * DV_GLA)), _const_spec((1, DV_GLA))],
        out_specs=[v_spec, pl.BlockSpec((1, 2, DK_GLA, DV_GLA), lambda b, p, c: (b, p, 0, 0))],
        out_shape=[jax.ShapeDtypeStruct((batch * seq, GLA_V_W), BF16),
                   jax.ShapeDtypeStruct((batch, H_GLA, DK_GLA, DV_GLA), F32)],
        scratch_shapes=[pltpu.VMEM((2 * DV_GLA, LANES), F32)],
        compiler_params=_cparams(("parallel", "parallel", "arbitrary")),
        name="gla_prompt",
    )(qg, kg, la, vg, gg, wts['tri'], wts['ind'], wts['gla_g'])


def _gla_step_kernel(q_ref, k_ref, la_ref, v_ref, gate_ref, s_ref, gn_ref, o_ref, so_ref):
    b = pl.program_id(0)
    nb = q_ref.shape[0]

    def column(ref):
        x = ref[...]
        pad = jnp.zeros((LANES - nb, x.shape[1]), F32)
        xt = jnp.concatenate([x, pad], axis=0).T
        lane = lax.broadcasted_iota(jnp.int32, xt.shape, 1)
        return jnp.sum(jnp.where(lane == b, xt, 0.0), axis=1, keepdims=True)

    qc = column(q_ref)
    kc = column(k_ref)
    ac = jnp.exp(column(la_ref))
    v = v_ref[0]
    gate = gate_ref[0]
    gn = gn_ref[...]
    outs = []
    for h in range(H_GLA):
        sl = slice(h * DK_GLA, (h + 1) * DK_GLA)
        vs = slice(h * DV_GLA, (h + 1) * DV_GLA)
        s_new = s_ref[0, h] * ac[sl] + kc[sl] * v[:, vs]
        so_ref[0, h] = s_new
        oh = jnp.sum(qc[sl] * s_new, axis=0, keepdims=True)
        r = lax.rsqrt(jnp.mean(oh * oh, axis=-1, keepdims=True) + EPS)
        gh = gate[:, vs]
        outs.append(((oh * r) * gn) * (gh * (1.0 / (1.0 + jnp.exp(-gh)))))
    o_ref[0] = jnp.concatenate(outs, axis=1)


def _gla_step(qg, kg, la, vg, gg, state, wts):
    nb = qg.shape[0]
    assert nb <= LANES
    full = _const_spec((nb, GLA_QK_W))
    row3 = pl.BlockSpec((1, 1, GLA_V_W), lambda b: (b, 0, 0))
    st_spec = pl.BlockSpec((1, H_GLA, DK_GLA, DV_GLA), lambda b: (b, 0, 0, 0))
    o, s = pl.pallas_call(
        _gla_step_kernel,
        grid=(nb,),
        in_specs=[full, full, full, row3, row3, st_spec, _const_spec((1, DV_GLA))],
        out_specs=[row3, st_spec],
        out_shape=[jax.ShapeDtypeStruct((nb, 1, GLA_V_W), F32),
                   jax.ShapeDtypeStruct(state.shape, F32)],
        compiler_params=_cparams(("parallel",)),
        name="gla_step",
    )(qg, kg, la, vg.reshape(nb, 1, GLA_V_W), gg.reshape(nb, 1, GLA_V_W), state, wts['gla_g'])
    return o.reshape(nb, GLA_V_W), s


def _diff_finish(o1, o2, lam, dn_g):
    o = o1 - lam * o2
    r = lax.rsqrt(jnp.mean(o * o, axis=-1, keepdims=True) + EPS)
    return ((o * r) * dn_g) * (1.0 - LAMBDA_INIT)


def _attn_prompt_kernel(q_ref, k_ref, v_ref, lq1, lk1, lq2, lk2, dn_ref, o_ref,
                        m_ref, l_ref, acc_ref):
    qi = pl.program_id(2)
    T = q_ref.shape[0]
    q = q_ref[...]
    lane = lax.broadcasted_iota(jnp.int32, (T, LANES), 1)
    zero = jnp.zeros_like(q)
    qq = jnp.concatenate([jnp.where(lane < DH_DIFF, q, zero),
                          jnp.where(lane < DH_DIFF, zero, q)], axis=0)

    start = pl.multiple_of(qi * T, T)
    s = _dot_nt(qq, k_ref[pl.ds(start, T), :])
    row = lax.broadcasted_iota(jnp.int32, (2 * T, T), 0) & (T - 1)
    col = lax.broadcasted_iota(jnp.int32, (2 * T, T), 1)
    s = jnp.where(col <= row, s, -jnp.inf)
    m0 = jnp.max(s, axis=-1, keepdims=True)
    p = jnp.exp2(s - m0)
    m_ref[...] = m0
    l_ref[...] = jnp.sum(p, axis=-1, keepdims=True)
    acc_ref[...] = _dot(p.astype(BF16), v_ref[pl.ds(start, T), :])

    def body(ki, carry):
        off = pl.multiple_of(ki * T, T)
        s = _dot_nt(qq, k_ref[pl.ds(off, T), :])
        m_old = m_ref[...]
        m_new = jnp.maximum(m_old, jnp.max(s, axis=-1, keepdims=True))
        alpha = jnp.exp2(m_old - m_new)
        p = jnp.exp2(s - m_new)
        l_ref[...] = alpha * l_ref[...] + jnp.sum(p, axis=-1, keepdims=True)
        acc_ref[...] = alpha * acc_ref[...] + _dot(p.astype(BF16), v_ref[pl.ds(off, T), :])
        m_ref[...] = m_new
        return carry

    lax.fori_loop(0, qi, body, 0)

    lam = _lambda_full(lq1[...], lk1[...], lq2[...], lk2[...])
    on = acc_ref[...] / l_ref[...]
    o_ref[...] = _diff_finish(on[:T], on[T:], lam, dn_ref[...]).astype(o_ref.dtype)


def _attn_prompt(qd, kd, vd, wts, batch, seq):
    T = ATT_T
    assert seq % T == 0 and (T & (T - 1)) == 0
    nq = seq // T
    q_spec = pl.BlockSpec((T, LANES), lambda b, h, i: (b * nq + i, h))
    kv_spec = pl.BlockSpec((seq, LANES), lambda b, h, i: (b, h))
    lam_spec = _const_spec((1, DH_DIFF))
    return pl.pallas_call(
        _attn_prompt_kernel,
        grid=(batch, H_DIFF, nq),
        in_specs=[q_spec, kv_spec, kv_spec, lam_spec, lam_spec, lam_spec, lam_spec,
                  _const_spec((1, DV_DIFF))],
        out_specs=q_spec,
        out_shape=jax.ShapeDtypeStruct((batch * seq, DIFF_V_W), BF16),
        scratch_shapes=[pltpu.VMEM((2 * T, 1), F32), pltpu.VMEM((2 * T, 1), F32),
                        pltpu.VMEM((2 * T, DV_DIFF), F32)],
        compiler_params=_cparams(("parallel", "parallel", "arbitrary")),
        name="attn_prompt",
    )(qd, kd, vd, wts['lq1'], wts['lk1'], wts['lq2'], wts['lk2'], wts['dn_g'])


def _attn_decode_kernel(pt_ref, q_ref, kn_ref, vn_ref, lq1, lk1, lq2, lk2, dn_ref, *rest):
    np_ = DEC_PAGES
    k_refs = rest[:np_]
    v_refs = rest[np_:2 * np_]
    o_ref, qbd_ref, m_ref, l_ref, acc_ref = rest[2 * np_:]
    j = pl.program_id(1)
    R = 2 * H_DIFF
    rowi = lax.broadcasted_iota(jnp.int32, (R, DIFF_QK_W), 0)
    lanei = lax.broadcasted_iota(jnp.int32, (R, DIFF_QK_W), 1)

    @pl.when(j == 0)
    def _():
        qb = jnp.broadcast_to(q_ref[0], (R, DIFF_QK_W))
        qbd_ref[...] = jnp.where(lanei // DH_DIFF == rowi, qb, 0.0).astype(BF16)
        m_ref[...] = jnp.full_like(m_ref, NEG_BIG)
        l_ref[...] = jnp.zeros_like(l_ref)
        acc_ref[...] = jnp.zeros_like(acc_ref)

    qbd = qbd_ref[...]
    s = jnp.concatenate([_dot_nt(qbd, k_refs[i][0].astype(BF16)) for i in range(np_)], axis=1)
    m_old = m_ref[...]
    m_new = jnp.maximum(m_old, jnp.max(s, axis=-1, keepdims=True))
    alpha = jnp.exp2(m_old - m_new)
    p = jnp.exp2(s - m_new)
    l_new = alpha * l_ref[...] + jnp.sum(p, axis=-1, keepdims=True)
    pb = p.astype(BF16)
    acc_ref[...] = alpha * acc_ref[...]
    for i in range(np_):
        acc_ref[...] += _dot(pb[:, i * LANES:(i + 1) * LANES], v_refs[i][0].astype(BF16))
    acc_new = acc_ref[...]
    m_ref[...] = m_new
    l_ref[...] = l_new

    @pl.when(j == pl.num_programs(1) - 1)
    def _():
        s_n = jnp.sum(qbd.astype(F32) * kn_ref[0], axis=-1, keepdims=True)
        m_f = jnp.maximum(m_new, s_n)
        a_f = jnp.exp2(m_new - m_f)
        p_n = jnp.exp2(s_n - m_f)
        l_f = a_f * l_new + p_n
        acc_f = a_f * acc_new + p_n * vn_ref[0]
        lam = _lambda_full(lq1[...], lk1[...], lq2[...], lk2[...])
        coef = jnp.where((rowi & 1) == 0, 1.0, -lam)
        w = jnp.where(lanei // DV_DIFF == rowi // 2, (acc_f / l_f) * coef, 0.0)
        o = jnp.sum(w, axis=0, keepdims=True)
        dn = dn_ref[...]
        outs = []
        for h in range(H_DIFF):
            oh = o[:, h * DV_DIFF:(h + 1) * DV_DIFF]
            outs.append(_diff_finish(oh, jnp.zeros_like(oh), lam, dn))
        o_ref[0] = jnp.concatenate(outs, axis=1)


def _attn_decode(qd, kd_new, vd_new, cache_k, cache_v, page_table, wts):
    nb, n_pages = page_table.shape
    n_pool, page = cache_k.shape[0], cache_k.shape[1]
    assert n_pages % DEC_PAGES == 0 and page == LANES
    nj = n_pages // DEC_PAGES
    row3 = pl.BlockSpec((1, 1, DIFF_QK_W), lambda b, j, pt: (b, 0, 0))

    def page_spec(i):
        return pl.BlockSpec((1, page, DIFF_QK_W),
                            lambda b, j, pt: (pt[b * n_pages + j * DEC_PAGES + i], 0, 0))

    lam_spec = pl.BlockSpec((1, DH_DIFF), lambda b, j, pt: (0, 0))
    grid_spec = pltpu.PrefetchScalarGridSpec(
        num_scalar_prefetch=1,
        grid=(nb, nj),
        in_specs=[row3, row3, row3, lam_spec, lam_spec, lam_spec, lam_spec,
                  pl.BlockSpec((1, DV_DIFF), lambda b, j, pt: (0, 0))]
                 + [page_spec(i) for i in range(DEC_PAGES)] * 2,
        out_specs=row3,
        scratch_shapes=[pltpu.VMEM((2 * H_DIFF, DIFF_QK_W), BF16),
                        pltpu.VMEM((2 * H_DIFF, 1), F32), pltpu.VMEM((2 * H_DIFF, 1), F32),
                        pltpu.VMEM((2 * H_DIFF, DIFF_V_W), F32)],
    )
    o = pl.pallas_call(
        _attn_decode_kernel,
        grid_spec=grid_spec,
        out_shape=jax.ShapeDtypeStruct((nb, 1, DIFF_V_W), F32),
        compiler_params=_cparams(("parallel", "arbitrary")),
        name="attn_decode",
    )(page_table.reshape(-1), qd.astype(F32).reshape(nb, 1, DIFF_QK_W),
      kd_new.reshape(nb, 1, DIFF_QK_W), vd_new.reshape(nb, 1, DIFF_V_W),
      wts['lq1'], wts['lk1'], wts['lq2'], wts['lk2'], wts['dn_g'],
      *([cache_k] * DEC_PAGES), *([cache_v] * DEC_PAGES))
    return o.reshape(nb, DIFF_V_W)


def _out_ffn_kernel(x_ref, mg_ref, md_ref, wo_ref, g_ref, wup_ref, wdn_ref, y_ref, *, fc):
    half = GLA_V_W
    x1 = (x_ref[...] + _dot(mg_ref[...].astype(BF16), wo_ref[0:half, :])
          + _dot(md_ref[...].astype(BF16), wo_ref[half:, :]))
    r = lax.rsqrt(jnp.mean(x1 * x1, axis=-1, keepdims=True) + EPS)
    h2 = ((x1 * r) * g_ref[...]).astype(BF16)
    y_ref[...] = x1
    for f in range(D_FF // fc):
        u = jnp.maximum(_dot(h2, wup_ref[:, f * fc:(f + 1) * fc]), 0.0)
        y_ref[...] += _dot((u * u).astype(BF16), wdn_ref[f * fc:(f + 1) * fc, :])


def _out_ffn(x, mix_g, mix_d, wts, tm, fc=512):
    n = x.shape[0]
    assert n % tm == 0
    row = lambda w: pl.BlockSpec((tm, w), lambda i: (i, 0))
    consts = [wts['w_out'], wts['g_ffn'], wts['w_up'], wts['w_down']]
    const_specs = [_const_spec(c.shape) for c in consts]
    return pl.pallas_call(
        functools.partial(_out_ffn_kernel, fc=fc),
        grid=(n // tm,),
        in_specs=[row(D_MODEL), row(GLA_V_W), row(DIFF_V_W)] + const_specs,
        out_specs=row(D_MODEL),
        out_shape=jax.ShapeDtypeStruct((n, D_MODEL), F32),
        compiler_params=_cparams(("parallel",)),
        name="out_ffn",
    )(x, mix_g, mix_d, *consts)


def _prep_weights(norm_mix_g, w_in, w_gk2, b_gk, gla_norm_g, q_norm_g, k_norm_g,
                  lambda_q1, lambda_k1, lambda_q2, lambda_k2, diff_norm_g, w_out,
                  norm_ffn_g, w_up, w_down):
    w = w_in[0]
    blk = np.arange(2 * LANES) // DH_DIFF
    ind = (np.arange(LANES)[:, None] // DK_GLA) == (np.arange(2 * DV_GLA)[None, :] // DV_GLA)
    return {
        'g_mix': norm_mix_g[0][None, :],
        'w_gla': w[:, :OFF_GLR].astype(BF16),
        'w_glr': jnp.pad(w[:, OFF_GLR:OFF_DIFF], ((0, 0), (0, LANES - GK_RANK))).astype(BF16),
        'w_gk2': jnp.pad(w_gk2[0], ((0, LANES - GK_RANK), (0, 0))).astype(BF16),
        'b_gk': b_gk[0][None, :],
        'w_diff': w[:, OFF_DIFF:].astype(BF16),
        'qn_g': jnp.tile(q_norm_g[0], 2 * H_DIFF)[None, :],
        'kn_g': jnp.tile(k_norm_g[0], 2 * H_DIFF)[None, :],
        'ones_blk': jnp.asarray(blk[:, None] == blk[None, :], BF16),
        'tri': jnp.asarray(np.tril(np.ones((GLA_C, GLA_C))), BF16),
        'ind': jnp.asarray(ind, BF16),
        'gla_g': gla_norm_g[0][None, :],
        'lq1': lambda_q1[0][None, :], 'lk1': lambda_k1[0][None, :],
        'lq2': lambda_q2[0][None, :], 'lk2': lambda_k2[0][None, :],
        'dn_g': diff_norm_g[0][None, :],
        'w_out': w_out[0].astype(BF16),
        'g_ffn': norm_ffn_g[0][None, :],
        'w_up': w_up[0].astype(BF16),
        'w_down': w_down[0].astype(BF16),
    }


def kernel(x_prompt, x_sample, cache_k, cache_v, page_table, state_gla, norm_mix_g, w_in, w_gk2, b_gk, gla_norm_g, q_norm_g, k_norm_g, lambda_q1, lambda_k1, lambda_q2, lambda_k2, diff_norm_g, w_out, norm_ffn_g, w_up, w_down):
    assert w_in.shape[0] == 1, "single-layer trunk"
    B, T, D = x_prompt.shape
    Bd, Td, _ = x_sample.shape
    assert Td == 1
    wts = _prep_weights(norm_mix_g, w_in, w_gk2, b_gk, gla_norm_g, q_norm_g, k_norm_g,
                        lambda_q1, lambda_k1, lambda_q2, lambda_k2, diff_norm_g, w_out,
                        norm_ffn_g, w_up, w_down)

    xp = x_prompt.reshape(B * T, D)
    qg, kg, vg, gg, la, qd, kdf, kdb, vdf, vdb = _in_proj(xp, wts, tm=256)
    mix_g, s_prompt = _gla_prompt(qg, kg, la, vg, gg, wts, B, T)
    mix_d = _attn_prompt(qd, kdb, vdb, wts, B, T)
    y_prompt = _out_ffn(xp, mix_g, mix_d, wts, tm=256)

    xs = x_sample.reshape(Bd, D)
    qg_s, kg_s, vg_s, gg_s, la_s, qd_s, kdf_s, _, vdf_s, _ = _in_proj(xs, wts, tm=Bd)
    mixg_s, s_sample = _gla_step(qg_s, kg_s, la_s, vg_s, gg_s, state_gla[0], wts)
    n_pool, page = cache_k.shape[1], cache_k.shape[2]
    mixd_s = _attn_decode(qd_s, kdf_s, vdf_s,
                          cache_k[0].reshape(n_pool, page, DIFF_QK_W),
                          cache_v[0].reshape(n_pool, page, DIFF_V_W), page_table, wts)
    y_sample = _out_ffn(xs, mixg_s, mixd_s, wts, tm=Bd)

    return (y_prompt.reshape(B, T, D),
            y_sample.reshape(Bd, Td, D),
            kdf.reshape(1, B, T, H_DIFF, 2, DH_DIFF),
            vdf.reshape(1, B, T, H_DIFF, DV_DIFF),
            s_prompt[None],
            kdf_s.reshape(1, Bd, Td, H_DIFF, 2, DH_DIFF),
            vdf_s.reshape(1, Bd, Td, H_DIFF, DV_DIFF),
            s_sample[None])
```

```python
import functools
import math

import jax
import jax.numpy as jnp
import numpy as np
from jax import lax
from jax.experimental import pallas as pl
from jax.experimental.pallas import tpu as pltpu

F32 = jnp.float32
BF16 = jnp.bfloat16

LANES = 128
SUBLANES = 8
VMEM_LIMIT = 56 * 1024 * 1024

D_MODEL = 1024
H_GLA = 4
DK_GLA = 64
DV_GLA = 128
GK_RANK = 16
GK_NORMALIZER = 16.0
H_DIFF = 4
DH_DIFF = 64
DV_DIFF = 128
D_FF = 4 * D_MODEL
EPS = 1e-6
LAMBDA_INIT = 0.8 - 0.6 * math.exp(-0.3 * 0)

GLA_QK_W = H_GLA * DK_GLA
GLA_V_W = H_GLA * DV_GLA
DIFF_QK_W = H_DIFF * 2 * DH_DIFF
DIFF_V_W = H_DIFF * DV_DIFF
OFF_GLR = 2 * GLA_QK_W + 2 * GLA_V_W
OFF_DIFF = OFF_GLR + GK_RANK

LOG2E = 1.4426950408889634
NEG_BIG = -1e30

GLA_C = 256
GLA_DIAG = 8
ATT_T = 512
DEC_PAGES = 8


def _cparams(sem):
    return pltpu.CompilerParams(dimension_semantics=sem, vmem_limit_bytes=VMEM_LIMIT)


def _const_spec(shape):
    nd = len(shape)
    return pl.BlockSpec(shape, lambda *_: (0,) * nd)


def _dot(a, b):
    return jnp.dot(a, b, preferred_element_type=F32)


def _dot_nt(a, b):
    return lax.dot_general(a, b, (((1,), (1,)), ((), ())), preferred_element_type=F32)


def _lambda_full(lq1, lk1, lq2, lk2):
    a = jnp.sum(lq1 * lk1, axis=-1, keepdims=True)
    b = jnp.sum(lq2 * lk2, axis=-1, keepdims=True)
    return jnp.exp(a) - jnp.exp(b) + LAMBDA_INIT


def _group_rms(z, ones_blk, group):
    outs = []
    w = ones_blk.shape[0]
    for c in range(z.shape[1] // w):
        zc = z[:, c * w:(c + 1) * w]
        outs.append(_dot((zc * zc).astype(BF16), ones_blk))
    ss = jnp.concatenate(outs, axis=1)
    return lax.rsqrt(ss * (1.0 / group) + EPS)


def _in_proj_kernel(x_ref, g_ref, wgla_ref, wglr_ref, wgk2_ref, bgk_ref, wdiff_ref,
                    qng_ref, kng_ref, ones_ref,
                    qg_ref, kg_ref, vg_ref, gg_ref, la_ref,
                    qd_ref, kdf_ref, kdb_ref, vdf_ref, vdb_ref):
    x = x_ref[...]
    r = lax.rsqrt(jnp.mean(x * x, axis=-1, keepdims=True) + EPS)
    h = ((x * r) * g_ref[...]).astype(BF16)

    qg_ref[...] = _dot(h, wgla_ref[:, 0:GLA_QK_W]) * (DK_GLA ** -0.5)
    kg_ref[...] = _dot(h, wgla_ref[:, GLA_QK_W:2 * GLA_QK_W])
    vg_ref[...] = _dot(h, wgla_ref[:, 2 * GLA_QK_W:2 * GLA_QK_W + GLA_V_W])
    gg_ref[...] = _dot(h, wgla_ref[:, 2 * GLA_QK_W + GLA_V_W:])

    glr = _dot(h, wglr_ref[...])
    gl = _dot(glr.astype(BF16), wgk2_ref[...]) + bgk_ref[...]
    log_sig = jnp.minimum(gl, 0.0) - jnp.log1p(jnp.exp(-jnp.abs(gl)))
    la_ref[...] = log_sig / GK_NORMALIZER

    ones_blk = ones_ref[...]
    dq = _dot(h, wdiff_ref[:, 0:DIFF_QK_W])
    qn = (dq * _group_rms(dq, ones_blk, DH_DIFF)) * qng_ref[...]
    qd_ref[...] = (qn * (DH_DIFF ** -0.5 * LOG2E)).astype(BF16)
    dk = _dot(h, wdiff_ref[:, DIFF_QK_W:2 * DIFF_QK_W])
    kn = (dk * _group_rms(dk, ones_blk, DH_DIFF)) * kng_ref[...]
    kdf_ref[...] = kn
    kdb_ref[...] = kn.astype(BF16)
    dv = _dot(h, wdiff_ref[:, 2 * DIFF_QK_W:])
    vdf_ref[...] = dv
    vdb_ref[...] = dv.astype(BF16)


def _in_proj(x, wts, tm):
    n = x.shape[0]
    assert n % tm == 0
    row = lambda w: pl.BlockSpec((tm, w), lambda i: (i, 0))
    out_w = [(GLA_QK_W, F32), (GLA_QK_W, F32), (GLA_V_W, F32), (GLA_V_W, F32), (GLA_QK_W, F32),
             (DIFF_QK_W, BF16), (DIFF_QK_W, F32), (DIFF_QK_W, BF16), (DIFF_V_W, F32), (DIFF_V_W, BF16)]
    consts = [wts['g_mix'], wts['w_gla'], wts['w_glr'], wts['w_gk2'], wts['b_gk'], wts['w_diff'],
              wts['qn_g'], wts['kn_g'], wts['ones_blk']]
    return pl.pallas_call(
        _in_proj_kernel,
        grid=(n // tm,),
        in_specs=[row(D_MODEL)] + [_const_spec(c.shape) for c in consts],
        out_specs=[row(w) for w, _ in out_w],
        out_shape=[jax.ShapeDtypeStruct((n, w), dt) for w, dt in out_w],
        compiler_params=_cparams(("parallel",)),
        name="in_proj",
    )(x, *consts)


def _bcast_row(x3, j):
    return jnp.broadcast_to(x3[:, j:j + 1, :], x3.shape)


def _gla_prompt_kernel(q_ref, k_ref, la_ref, v_ref, gate_ref, tri_ref, ind_ref, gn_ref,
                       o_ref, s_ref, st_ref):
    c = pl.program_id(2)
    C = q_ref.shape[0]
    neg_inf = -jnp.inf

    @pl.when(c == 0)
    def _():
        st_ref[...] = jnp.zeros_like(st_ref)

    la = la_ref[...]
    tri = tri_ref[...]
    t1 = la.astype(BF16)
    r1 = la - t1.astype(F32)
    t2 = r1.astype(BF16)
    t3 = (r1 - t2.astype(F32)).astype(BF16)
    b = _dot(tri, t1) + _dot(tri, t2) + _dot(tri, t3)

    q = q_ref[...]
    k = k_ref[...]
    v = v_ref[...]
    vb = v.astype(BF16)
    lane = lax.broadcasted_iota(jnp.int32, (C, LANES), 1)
    head0 = lane < DK_GLA

    G = C // GLA_DIAG
    b3 = b.reshape(G, GLA_DIAG, LANES)
    q3 = q.reshape(G, GLA_DIAG, LANES)
    k3 = k.reshape(G, GLA_DIAG, LANES)
    v3 = v.reshape(G, GLA_DIAG, 2 * DV_GLA)
    sub = lax.broadcasted_iota(jnp.int32, (G, GLA_DIAG, LANES), 1)
    ind = ind_ref[...]
    o = jnp.zeros((C, 2 * DV_GLA), F32)
    for j in range(GLA_DIAG):
        e = jnp.exp(jnp.where(sub >= j, b3 - _bcast_row(b3, j), neg_inf))
        pj = (q3 * _bcast_row(k3, j) * e).reshape(C, LANES).astype(BF16)
        rj = _dot(pj, ind)
        o = o + rj * _bcast_row(v3, j).reshape(C, 2 * DV_GLA)

    ts_xor = (lax.broadcasted_iota(jnp.int32, (C, C), 0)
              ^ lax.broadcasted_iota(jnp.int32, (C, C), 1))
    tok = lax.broadcasted_iota(jnp.int32, (C, LANES), 0)
    a0 = jnp.zeros((C, C), F32)
    a1 = jnp.zeros((C, C), F32)
    m = GLA_DIAG
    while m < C:
        P = C // (2 * m)
        bmid = _bcast_row(b.reshape(P, 2 * m, LANES), m).reshape(C, LANES)
        late = (tok & m) != 0
        qt = q * jnp.exp(jnp.where(late, b - bmid, neg_inf))
        kt = (k * jnp.exp(jnp.where(late, neg_inf, bmid - b))).astype(BF16)
        split_here = (ts_xor // m) == 1
        x0 = _dot_nt(jnp.where(head0, qt, 0.0).astype(BF16), kt)
        x1 = _dot_nt(jnp.where(head0, 0.0, qt).astype(BF16), kt)
        a0 = jnp.where(split_here, x0, a0)
        a1 = jnp.where(split_here, x1, a1)
        m *= 2
    o_intra = jnp.concatenate([_dot(a0.astype(BF16), vb[:, :DV_GLA]),
                               _dot(a1.astype(BF16), vb[:, DV_GLA:])], axis=1)

    st = st_ref[...]
    stb = st.astype(BF16)
    qe = q * jnp.exp(b)
    o_inter = jnp.concatenate(
        [_dot_nt(jnp.where(head0, qe, 0.0).astype(BF16), stb[:DV_GLA]),
         _dot_nt(jnp.where(head0, 0.0, qe).astype(BF16), stb[DV_GLA:])], axis=1)
    o = o + o_intra + o_inter

    b_last = b[C - 1:C, :]
    kd = (k * jnp.exp(b_last - b)).astype(BF16)
    kv0 = _dot(vb[:, :DV_GLA].T, kd)
    kv1 = _dot(vb[:, DV_GLA:].T, kd)
    head0_s = lax.broadcasted_iota(jnp.int32, (DV_GLA, LANES), 1) < DK_GLA
    dec = jnp.exp(b_last)
    st_new = jnp.concatenate([st[:DV_GLA] * dec + jnp.where(head0_s, kv0, 0.0),
                              st[DV_GLA:] * dec + jnp.where(head0_s, 0.0, kv1)], axis=0)
    st_ref[...] = st_new

    gate = gate_ref[...]
    gn = gn_ref[...]
    outs = []
    for hh in range(2):
        oh = o[:, hh * DV_GLA:(hh + 1) * DV_GLA]
        gh = gate[:, hh * DV_GLA:(hh + 1) * DV_GLA]
        r = lax.rsqrt(jnp.mean(oh * oh, axis=-1, keepdims=True) + EPS)
        outs.append(((oh * r) * gn) * (gh * (1.0 / (1.0 + jnp.exp(-gh)))))
    o_ref[...] = jnp.concatenate(outs, axis=1).astype(o_ref.dtype)

    @pl.when(c == pl.num_programs(2) - 1)
    def _():
        s0 = st_new[:DV_GLA].T
        s1 = st_new[DV_GLA:].T
        s_ref[0, 0] = s0[:DK_GLA]
        s_ref[0, 1] = s1[DK_GLA:]


def _gla_prompt(qg, kg, la, vg, gg, wts, batch, seq):
    C = GLA_C
    assert seq % C == 0
    nc = seq // C
    npair = H_GLA // 2
    qk_spec = pl.BlockSpec((C, LANES), lambda b, p, c: (b * nc + c, p))
    v_spec = pl.BlockSpec((C, 2 * DV_GLA), lambda b, p, c: (b * nc + c, p))
    return pl.pallas_call(
        _gla_prompt_kernel,
        grid=(batch, npair, nc),
        in_specs=[qk_spec, qk_spec, qk_spec, v_spec, v_spec,
                  _const_spec((C, C)), _const_spec((LANES, 2 * DV_GLA)), _const_spec((1, DV_GLA))],
        out_specs=[v_spec, pl.BlockSpec((1, 2, DK_GLA, DV_GLA), lambda b, p, c: (b, p, 0, 0))],
        out_shape=[jax.ShapeDtypeStruct((batch * seq, GLA_V_W), BF16),
                   jax.ShapeDtypeStruct((batch, H_GLA, DK_GLA, DV_GLA), F32)],
        scratch_shapes=[pltpu.VMEM((2 * DV_GLA, LANES), F32)],
        compiler_params=_cparams(("parallel", "parallel", "arbitrary")),
        name="gla_prompt",
    )(qg, kg, la, vg, gg, wts['tri'], wts['ind'], wts['gla_g'])


def _gla_step_kernel(q_ref, k_ref, la_ref, v_ref, gate_ref, s_ref, gn_ref, o_ref, so_ref):
    b = pl.program_id(0)
    nb = q_ref.shape[0]

    def column(ref):
        x = ref[...]
        pad = jnp.zeros((LANES - nb, x.shape[1]), F32)
        xt = jnp.concatenate([x, pad], axis=0).T
        lane = lax.broadcasted_iota(jnp.int32, xt.shape, 1)
        return jnp.sum(jnp.where(lane == b, xt, 0.0), axis=1, keepdims=True)

    qc = column(q_ref)
    kc = column(k_ref)
    ac = jnp.exp(column(la_ref))
    v = v_ref[0]
    gate = gate_ref[0]
    gn = gn_ref[...]
    outs = []
    for h in range(H_GLA):
        sl = slice(h * DK_GLA, (h + 1) * DK_GLA)
        vs = slice(h * DV_GLA, (h + 1) * DV_GLA)
        s_new = s_ref[0, h] * ac[sl] + kc[sl] * v[:, vs]
        so_ref[0, h] = s_new
        oh = jnp.sum(qc[sl] * s_new, axis=0, keepdims=True)
        r = lax.rsqrt(jnp.mean(oh * oh, axis=-1, keepdims=True) + EPS)
        gh = gate[:, vs]
        outs.append(((oh * r) * gn) * (gh * (1.0 / (1.0 + jnp.exp(-gh)))))
    o_ref[0] = jnp.concatenate(outs, axis=1)


def _gla_step(qg, kg, la, vg, gg, state, wts):
    nb = qg.shape[0]
    assert nb <= LANES
    full = _const_spec((nb, GLA_QK_W))
    row3 = pl.BlockSpec((1, 1, GLA_V_W), lambda b: (b, 0, 0))
    st_spec = pl.BlockSpec((1, H_GLA, DK_GLA, DV_GLA), lambda b: (b, 0, 0, 0))
    o, s = pl.pallas_call(
        _gla_step_kernel,
        grid=(nb,),
        in_specs=[full, full, full, row3, row3, st_spec, _const_spec((1, DV_GLA))],
        out_specs=[row3, st_spec],
        out_shape=[jax.ShapeDtypeStruct((nb, 1, GLA_V_W), F32),
                   jax.ShapeDtypeStruct(state.shape, F32)],
        compiler_params=_cparams(("parallel",)),
        name="gla_step",
    )(qg, kg, la, vg.reshape(nb, 1, GLA_V_W), gg.reshape(nb, 1, GLA_V_W), state, wts['gla_g'])
    return o.reshape(nb, GLA_V_W), s


def _diff_finish(o1, o2, lam, dn_g):
    o = o1 - lam * o2
    r = lax.rsqrt(jnp.mean(o * o, axis=-1, keepdims=True) + EPS)
    return ((o * r) * dn_g) * (1.0 - LAMBDA_INIT)


def _attn_prompt_kernel(q_ref, k_ref, v_ref, lq1, lk1, lq2, lk2, dn_ref, o_ref, m_ref, acc_ref):
    qi = pl.program_id(2)
    T = q_ref.shape[0]
    q = q_ref[...]
    lane = lax.broadcasted_iota(jnp.int32, (T, LANES), 1)
    zero = jnp.zeros_like(q)
    qs = (jnp.where(lane < DH_DIFF, q, zero), jnp.where(lane < DH_DIFF, zero, q))
    ones = jnp.ones((T, LANES), BF16)

    def keys_values(off):
        return (k_ref[pl.ds(off, T), :],
                jnp.concatenate([v_ref[pl.ds(off, T), :], ones], axis=1))

    k, v1 = keys_values(pl.multiple_of(qi * T, T))
    causal = (lax.broadcasted_iota(jnp.int32, (T, T), 1)
              <= lax.broadcasted_iota(jnp.int32, (T, T), 0))
    for c in range(2):
        s = jnp.where(causal, _dot_nt(qs[c], k), -jnp.inf)
        m0 = jnp.max(s, axis=-1, keepdims=True)
        m_ref[c] = jnp.broadcast_to(m0, (T, LANES))
        acc_ref[c] = _dot(jnp.exp2(s - m0).astype(BF16), v1)

    def body(ki, carry):
        k, v1 = keys_values(pl.multiple_of(ki * T, T))
        for c in range(2):
            s = _dot_nt(qs[c], k)
            m_old = m_ref[c]
            m_new = jnp.maximum(m_old, jnp.max(s, axis=-1, keepdims=True))
            alpha = jnp.exp2(m_old - m_new)
            p = jnp.exp2(s - pltpu.repeat(m_new, T // LANES, axis=1))
            acc_ref[c] = pltpu.repeat(alpha, 2, axis=1) * acc_ref[c] + _dot(p.astype(BF16), v1)
            m_ref[c] = m_new
        return carry

    lax.fori_loop(0, qi, body, 0)

    lam = _lambda_full(lq1[...], lk1[...], lq2[...], lk2[...])
    on = [acc_ref[c][:, :DV_DIFF] / acc_ref[c][:, DV_DIFF:] for c in range(2)]
    o_ref[...] = _diff_finish(on[0], on[1], lam, dn_ref[...]).astype(o_ref.dtype)


def _attn_prompt(qd, kd, vd, wts, batch, seq):
    T = ATT_T
    assert seq % T == 0 and (T & (T - 1)) == 0
    nq = seq // T
    q_spec = pl.BlockSpec((T, LANES), lambda b, h, i: (b * nq + i, h))
    kv_spec = pl.BlockSpec((seq, LANES), lambda b, h, i: (b, h))
    lam_spec = _const_spec((1, DH_DIFF))
    return pl.pallas_call(
        _attn_prompt_kernel,
        grid=(batch, H_DIFF, nq),
        in_specs=[q_spec, kv_spec, kv_spec, lam_spec, lam_spec, lam_spec, lam_spec,
                  _const_spec((1, DV_DIFF))],
        out_specs=q_spec,
        out_shape=jax.ShapeDtypeStruct((batch * seq, DIFF_V_W), BF16),
        scratch_shapes=[pltpu.VMEM((2, T, LANES), F32), pltpu.VMEM((2, T, 2 * DV_DIFF), F32)],
        compiler_params=_cparams(("parallel", "parallel", "arbitrary")),
        name="attn_prompt",
    )(qd, kd, vd, wts['lq1'], wts['lk1'], wts['lq2'], wts['lk2'], wts['dn_g'])


def _attn_decode_kernel(pt_ref, q_ref, kn_ref, vn_ref, lq1, lk1, lq2, lk2, dn_ref, *rest):
    np_ = DEC_PAGES
    k_refs = rest[:np_]
    v_refs = rest[np_:2 * np_]
    o_ref, qbd_ref, m_ref, l_ref, acc_ref = rest[2 * np_:]
    j = pl.program_id(1)
    R = 2 * H_DIFF
    rowi = lax.broadcasted_iota(jnp.int32, (R, DIFF_QK_W), 0)
    lanei = lax.broadcasted_iota(jnp.int32, (R, DIFF_QK_W), 1)

    @pl.when(j == 0)
    def _():
        qb = jnp.broadcast_to(q_ref[0], (R, DIFF_QK_W))
        qbd_ref[...] = jnp.where(lanei // DH_DIFF == rowi, qb, 0.0).astype(BF16)
        m_ref[...] = jnp.full_like(m_ref, NEG_BIG)
        l_ref[...] = jnp.zeros_like(l_ref)
        acc_ref[...] = jnp.zeros_like(acc_ref)

    qbd = qbd_ref[...]
    s = jnp.concatenate([_dot(qbd, k_refs[i][0].astype(BF16)) for i in range(np_)], axis=1)
    m_old = m_ref[...]
    m_new = jnp.maximum(m_old, jnp.max(s, axis=-1, keepdims=True))
    alpha = jnp.exp2(m_old - m_new)
    p = jnp.exp2(s - m_new)
    l_new = alpha * l_ref[...] + jnp.sum(p, axis=-1, keepdims=True)
    pb = p.astype(BF16)
    acc_ref[...] = alpha * acc_ref[...]
    for i in range(np_):
        pi = pb[:, i * LANES:(i + 1) * LANES]
        for h in range(H_DIFF):
            vh = v_refs[i].at[0][pl.ds(h, LANES, stride=H_DIFF), :]
            acc_ref[:, h * DV_DIFF:(h + 1) * DV_DIFF] += _dot(pi, vh.astype(BF16))
    acc_new = acc_ref[...]
    m_ref[...] = m_new
    l_ref[...] = l_new

    @pl.when(j == pl.num_programs(1) - 1)
    def _():
        s_n = jnp.sum(qbd.astype(F32) * kn_ref[0], axis=-1, keepdims=True)
        m_f = jnp.maximum(m_new, s_n)
        a_f = jnp.exp2(m_new - m_f)
        p_n = jnp.exp2(s_n - m_f)
        l_f = a_f * l_new + p_n
        acc_f = a_f * acc_new + p_n * vn_ref[0]
        lam = _lambda_full(lq1[...], lk1[...], lq2[...], lk2[...])
        coef = jnp.where((rowi & 1) == 0, 1.0, -lam)
        w = jnp.where(lanei // DV_DIFF == rowi // 2, (acc_f / l_f) * coef, 0.0)
        o = jnp.sum(w, axis=0, keepdims=True)
        dn = dn_ref[...]
        outs = []
        for h in range(H_DIFF):
            oh = o[:, h * DV_DIFF:(h + 1) * DV_DIFF]
            outs.append(_diff_finish(oh, jnp.zeros_like(oh), lam, dn))
        o_ref[0] = jnp.concatenate(outs, axis=1)


def _attn_decode(qd, kd_new, vd_new, cache_kt, cache_vr, page_table, wts):
    nb, n_pages = page_table.shape
    page = cache_kt.shape[2]
    assert n_pages % DEC_PAGES == 0 and page == LANES
    nj = n_pages // DEC_PAGES
    row3 = pl.BlockSpec((1, 1, DIFF_QK_W), lambda b, j, pt: (b, 0, 0))

    def page_spec(i):
        return pl.BlockSpec((1, DIFF_QK_W, LANES),
                            lambda b, j, pt: (pt[b * n_pages + j * DEC_PAGES + i], 0, 0))

    lam_spec = pl.BlockSpec((1, DH_DIFF), lambda b, j, pt: (0, 0))
    grid_spec = pltpu.PrefetchScalarGridSpec(
        num_scalar_prefetch=1,
        grid=(nb, nj),
        in_specs=[row3, row3, row3, lam_spec, lam_spec, lam_spec, lam_spec,
                  pl.BlockSpec((1, DV_DIFF), lambda b, j, pt: (0, 0))]
                 + [page_spec(i) for i in range(DEC_PAGES)] * 2,
        out_specs=row3,
        scratch_shapes=[pltpu.VMEM((2 * H_DIFF, DIFF_QK_W), BF16),
                        pltpu.VMEM((2 * H_DIFF, 1), F32), pltpu.VMEM((2 * H_DIFF, 1), F32),
                        pltpu.VMEM((2 * H_DIFF, DIFF_V_W), F32)],
    )
    o = pl.pallas_call(
        _attn_decode_kernel,
        grid_spec=grid_spec,
        out_shape=jax.ShapeDtypeStruct((nb, 1, DIFF_V_W), F32),
        compiler_params=_cparams(("parallel", "arbitrary")),
        name="attn_decode",
    )(page_table.reshape(-1), qd.astype(F32).reshape(nb, 1, DIFF_QK_W),
      kd_new.reshape(nb, 1, DIFF_QK_W), vd_new.reshape(nb, 1, DIFF_V_W),
      wts['lq1'], wts['lk1'], wts['lq2'], wts['lk2'], wts['dn_g'],
      *([cache_kt] * DEC_PAGES), *([cache_vr] * DEC_PAGES))
    return o.reshape(nb, DIFF_V_W)


def _out_ffn_kernel(x_ref, mg_ref, md_ref, wo_ref, g_ref, wup_ref, wdn_ref, y_ref, *, fc):
    half = GLA_V_W
    x1 = (x_ref[...] + _dot(mg_ref[...].astype(BF16), wo_ref[0:half, :])
          + _dot(md_ref[...].astype(BF16), wo_ref[half:, :]))
    r = lax.rsqrt(jnp.mean(x1 * x1, axis=-1, keepdims=True) + EPS)
    h2 = ((x1 * r) * g_ref[...]).astype(BF16)
    y_ref[...] = x1
    for f in range(D_FF // fc):
        u = jnp.maximum(_dot(h2, wup_ref[:, f * fc:(f + 1) * fc]), 0.0)
        y_ref[...] += _dot((u * u).astype(BF16), wdn_ref[f * fc:(f + 1) * fc, :])


def _out_ffn(x, mix_g, mix_d, wts, tm, fc=512):
    n = x.shape[0]
    assert n % tm == 0
    row = lambda w: pl.BlockSpec((tm, w), lambda i: (i, 0))
    consts = [wts['w_out'], wts['g_ffn'], wts['w_up'], wts['w_down']]
    const_specs = [_const_spec(c.shape) for c in consts]
    return pl.pallas_call(
        functools.partial(_out_ffn_kernel, fc=fc),
        grid=(n // tm,),
        in_specs=[row(D_MODEL), row(GLA_V_W), row(DIFF_V_W)] + const_specs,
        out_specs=row(D_MODEL),
        out_shape=jax.ShapeDtypeStruct((n, D_MODEL), F32),
        compiler_params=_cparams(("parallel",)),
        name="out_ffn",
    )(x, mix_g, mix_d, *consts)


def _prep_weights(norm_mix_g, w_in, w_gk2, b_gk, gla_norm_g, q_norm_g, k_norm_g,
                  lambda_q1, lambda_k1, lambda_q2, lambda_k2, diff_norm_g, w_out,
                  norm_ffn_g, w_up, w_down):
    w = w_in[0]
    blk = np.arange(2 * LANES) // DH_DIFF
    ind = (np.arange(LANES)[:, None] // DK_GLA) == (np.arange(2 * DV_GLA)[None, :] // DV_GLA)
    return {
        'g_mix': norm_mix_g[0][None, :],
        'w_gla': w[:, :OFF_GLR].astype(BF16),
        'w_glr': jnp.pad(w[:, OFF_GLR:OFF_DIFF], ((0, 0), (0, LANES - GK_RANK))).astype(BF16),
        'w_gk2': jnp.pad(w_gk2[0], ((0, LANES - GK_RANK), (0, 0))).astype(BF16),
        'b_gk': b_gk[0][None, :],
        'w_diff': w[:, OFF_DIFF:].astype(BF16),
        'qn_g': jnp.tile(q_norm_g[0], 2 * H_DIFF)[None, :],
        'kn_g': jnp.tile(k_norm_g[0], 2 * H_DIFF)[None, :],
        'ones_blk': jnp.asarray(blk[:, None] == blk[None, :], BF16),
        'tri': jnp.asarray(np.tril(np.ones((GLA_C, GLA_C))), BF16),
        'ind': jnp.asarray(ind, BF16),
        'gla_g': gla_norm_g[0][None, :],
        'lq1': lambda_q1[0][None, :], 'lk1': lambda_k1[0][None, :],
        'lq2': lambda_q2[0][None, :], 'lk2': lambda_k2[0][None, :],
        'dn_g': diff_norm_g[0][None, :],
        'w_out': w_out[0].astype(BF16),
        'g_ffn': norm_ffn_g[0][None, :],
        'w_up': w_up[0].astype(BF16),
        'w_down': w_down[0].astype(BF16),
    }


def kernel(x_prompt, x_sample, cache_k, cache_v, page_table, state_gla, norm_mix_g, w_in, w_gk2, b_gk, gla_norm_g, q_norm_g, k_norm_g, lambda_q1, lambda_k1, lambda_q2, lambda_k2, diff_norm_g, w_out, norm_ffn_g, w_up, w_down):
    assert w_in.shape[0] == 1, "single-layer trunk"
    B, T, D = x_prompt.shape
    Bd, Td, _ = x_sample.shape
    assert Td == 1
    wts = _prep_weights(norm_mix_g, w_in, w_gk2, b_gk, gla_norm_g, q_norm_g, k_norm_g,
                        lambda_q1, lambda_k1, lambda_q2, lambda_k2, diff_norm_g, w_out,
                        norm_ffn_g, w_up, w_down)

    xp = x_prompt.reshape(B * T, D)
    qg, kg, vg, gg, la, qd, kdf, kdb, vdf, vdb = _in_proj(xp, wts, tm=256)
    mix_g, s_prompt = _gla_prompt(qg, kg, la, vg, gg, wts, B, T)
    mix_d = _attn_prompt(qd, kdb, vdb, wts, B, T)
    y_prompt = _out_ffn(xp, mix_g, mix_d, wts, tm=256)

    xs = x_sample.reshape(Bd, D)
    qg_s, kg_s, vg_s, gg_s, la_s, qd_s, kdf_s, _, vdf_s, _ = _in_proj(xs, wts, tm=Bd)
    mixg_s, s_sample = _gla_step(qg_s, kg_s, la_s, vg_s, gg_s, state_gla[0], wts)
    n_pool, page = cache_k.shape[1], cache_k.shape[2]
    cache_kt = jnp.transpose(cache_k[0], (0, 2, 3, 4, 1)).reshape(n_pool, DIFF_QK_W, page)
    cache_vr = cache_v[0].reshape(n_pool, page * H_DIFF, DV_DIFF)
    mixd_s = _attn_decode(qd_s, kdf_s, vdf_s, cache_kt, cache_vr, page_table, wts)
    y_sample = _out_ffn(xs, mixg_s, mixd_s, wts, tm=Bd)

    return (y_prompt.reshape(B, T, D),
            y_sample.reshape(Bd, Td, D),
            kdf.reshape(1, B, T, H_DIFF, 2, DH_DIFF),
            vdf.reshape(1, B, T, H_DIFF, DV_DIFF),
            s_prompt[None],
            kdf_s.reshape(1, Bd, Td, H_DIFF, 2, DH_DIFF),
            vdf_s.reshape(1, Bd, Td, H_DIFF, DV_DIFF),
            s_sample[None])
```

```python
import functools
import math

import jax
import jax.numpy as jnp
import numpy as np
from jax import lax
from jax.experimental import pallas as pl
from jax.experimental.pallas import tpu as pltpu

F32 = jnp.float32
BF16 = jnp.bfloat16

LANES = 128
SUBLANES = 8
VMEM_LIMIT = 56 * 1024 * 1024

D_MODEL = 1024
H_GLA = 4
DK_GLA = 64
DV_GLA = 128
GK_RANK = 16
GK_NORMALIZER = 16.0
H_DIFF = 4
DH_DIFF = 64
DV_DIFF = 128
D_FF = 4 * D_MODEL
EPS = 1e-6
LAMBDA_INIT = 0.8 - 0.6 * math.exp(-0.3 * 0)

GLA_QK_W = H_GLA * DK_GLA
GLA_V_W = H_GLA * DV_GLA
DIFF_QK_W = H_DIFF * 2 * DH_DIFF
DIFF_V_W = H_DIFF * DV_DIFF
OFF_GLR = 2 * GLA_QK_W + 2 * GLA_V_W
OFF_DIFF = OFF_GLR + GK_RANK

LOG2E = 1.4426950408889634
NEG_BIG = -1e30

GLA_C = 256
GLA_DIAG = 8
ATT_T = 512
TOK_TILE = 512
DEC_PAGES = 32


def _cparams(sem):
    return pltpu.CompilerParams(dimension_semantics=sem, vmem_limit_bytes=VMEM_LIMIT)


def _const_spec(shape):
    nd = len(shape)
    return pl.BlockSpec(shape, lambda *_: (0,) * nd)


def _resident_spec(shape):
    nd = len(shape)
    return pl.BlockSpec(shape, lambda *_: (0,) * nd, pipeline_mode=pl.Buffered(1))


def _dot(a, b):
    return jnp.dot(a, b, preferred_element_type=F32)


def _dot_nt(a, b):
    return lax.dot_general(a, b, (((1,), (1,)), ((), ())), preferred_element_type=F32)


def _lambda_full(lq1, lk1, lq2, lk2):
    a = jnp.sum(lq1 * lk1, axis=-1, keepdims=True)
    b = jnp.sum(lq2 * lk2, axis=-1, keepdims=True)
    return jnp.exp(a) - jnp.exp(b) + LAMBDA_INIT


def _group_rms(z, ones_blk, group):
    outs = []
    w = ones_blk.shape[0]
    for c in range(z.shape[1] // w):
        zc = z[:, c * w:(c + 1) * w]
        outs.append(_dot((zc * zc).astype(BF16), ones_blk))
    ss = jnp.concatenate(outs, axis=1)
    return lax.rsqrt(ss * (1.0 / group) + EPS)


def _in_proj_kernel(x_ref, g_ref, wgla_ref, wglr_ref, wgk2_ref, bgk_ref, wdiff_ref,
                    qng_ref, kng_ref, ones_ref,
                    qg_ref, kg_ref, vg_ref, gg_ref, la_ref,
                    qd_ref, kdf_ref, kdb_ref, vdf_ref, vdb_ref, *, cache_layout):
    x = x_ref[...]
    r = lax.rsqrt(jnp.mean(x * x, axis=-1, keepdims=True) + EPS)
    h = ((x * r) * g_ref[...]).astype(BF16)

    qg_ref[...] = _dot(h, wgla_ref[:, 0:GLA_QK_W]) * (DK_GLA ** -0.5)
    kg_ref[...] = _dot(h, wgla_ref[:, GLA_QK_W:2 * GLA_QK_W])
    vg_ref[...] = _dot(h, wgla_ref[:, 2 * GLA_QK_W:2 * GLA_QK_W + GLA_V_W])
    gg_ref[...] = _dot(h, wgla_ref[:, 2 * GLA_QK_W + GLA_V_W:])

    glr = _dot(h, wglr_ref[...])
    gl = _dot(glr.astype(BF16), wgk2_ref[...]) + bgk_ref[...]
    log_sig = jnp.minimum(gl, 0.0) - jnp.log1p(jnp.exp(-jnp.abs(gl)))
    la_ref[...] = log_sig / GK_NORMALIZER

    ones_blk = ones_ref[...]
    dq = _dot(h, wdiff_ref[:, 0:DIFF_QK_W])
    qn = (dq * _group_rms(dq, ones_blk, DH_DIFF)) * qng_ref[...]
    qd_ref[...] = (qn * (DH_DIFF ** -0.5 * LOG2E)).astype(BF16)
    dk = _dot(h, wdiff_ref[:, DIFF_QK_W:2 * DIFF_QK_W])
    kn = (dk * _group_rms(dk, ones_blk, DH_DIFF)) * kng_ref[...]
    dv = _dot(h, wdiff_ref[:, 2 * DIFF_QK_W:])
    vdb_ref[...] = dv.astype(BF16)
    if cache_layout:
        knt = kn.T
        kdf_ref[0] = knt
        kdb_ref[0] = knt.astype(BF16)
        tm = dv.shape[0]
        for hh in range(H_DIFF):
            vdf_ref[pl.ds(hh, tm, stride=H_DIFF), :] = dv[:, hh * DV_DIFF:(hh + 1) * DV_DIFF]
    else:
        kdf_ref[...] = kn
        kdb_ref[...] = kn.astype(BF16)
        vdf_ref[...] = dv


def _in_proj(x, wts, tm, seq=None):
    n = x.shape[0]
    assert n % tm == 0
    row = lambda w: pl.BlockSpec((tm, w), lambda i: (i, 0))
    out_w = [(GLA_QK_W, F32), (GLA_QK_W, F32), (GLA_V_W, F32), (GLA_V_W, F32), (GLA_QK_W, F32),
             (DIFF_QK_W, BF16), (DIFF_QK_W, F32), (DIFF_QK_W, BF16), (DIFF_V_W, F32), (DIFF_V_W, BF16)]
    out_specs = [row(w) for w, _ in out_w]
    out_shape = [jax.ShapeDtypeStruct((n, w), dt) for w, dt in out_w]
    if seq is not None:
        assert seq % tm == 0 and n % seq == 0
        per = seq // tm
        kt_spec = pl.BlockSpec((1, DIFF_QK_W, tm), lambda i: (i // per, 0, i % per))
        out_specs[6] = out_specs[7] = kt_spec
        out_shape[6] = jax.ShapeDtypeStruct((n // seq, DIFF_QK_W, seq), F32)
        out_shape[7] = jax.ShapeDtypeStruct((n // seq, DIFF_QK_W, seq), BF16)
        out_specs[8] = pl.BlockSpec((tm * H_DIFF, DV_DIFF), lambda i: (i, 0))
        out_shape[8] = jax.ShapeDtypeStruct((n * H_DIFF, DV_DIFF), F32)
    consts = [wts['g_mix'], wts['w_gla'], wts['w_glr'], wts['w_gk2'], wts['b_gk'], wts['w_diff'],
              wts['qn_g'], wts['kn_g'], wts['ones_blk']]
    return pl.pallas_call(
        functools.partial(_in_proj_kernel, cache_layout=seq is not None),
        grid=(n // tm,),
        in_specs=[row(D_MODEL)] + [_resident_spec(c.shape) for c in consts],
        out_specs=out_specs,
        out_shape=out_shape,
        compiler_params=_cparams(("parallel",)),
        name="in_proj",
    )(x, *consts)


def _bcast_row(x3, j):
    return jnp.broadcast_to(x3[:, j:j + 1, :], x3.shape)


def _gla_prompt_kernel(q_ref, k_ref, la_ref, v_ref, gate_ref, tri_ref, ind_ref, gn_ref,
                       o_ref, s_ref, st_ref):
    c = pl.program_id(2)
    C = q_ref.shape[0]
    neg_inf = -jnp.inf

    @pl.when(c == 0)
    def _():
        st_ref[...] = jnp.zeros_like(st_ref)

    la = la_ref[...]
    tri = tri_ref[...]
    t1 = la.astype(BF16)
    r1 = la - t1.astype(F32)
    t2 = r1.astype(BF16)
    t3 = (r1 - t2.astype(F32)).astype(BF16)
    b = _dot(tri, t1) + _dot(tri, t2) + _dot(tri, t3)

    q = q_ref[...]
    k = k_ref[...]
    v = v_ref[...]
    vb = v.astype(BF16)
    lane = lax.broadcasted_iota(jnp.int32, (C, LANES), 1)
    head0 = lane < DK_GLA

    G = C // GLA_DIAG
    b3 = b.reshape(G, GLA_DIAG, LANES)
    q3 = q.reshape(G, GLA_DIAG, LANES)
    k3 = k.reshape(G, GLA_DIAG, LANES)
    v3 = v.reshape(G, GLA_DIAG, 2 * DV_GLA)
    sub = lax.broadcasted_iota(jnp.int32, (G, GLA_DIAG, LANES), 1)
    ind = ind_ref[...]
    o = jnp.zeros((C, 2 * DV_GLA), F32)
    for j in range(GLA_DIAG):
        e = jnp.exp(jnp.where(sub >= j, b3 - _bcast_row(b3, j), neg_inf))
        pj = (q3 * _bcast_row(k3, j) * e).reshape(C, LANES).astype(BF16)
        rj = _dot(pj, ind)
        o = o + rj * _bcast_row(v3, j).reshape(C, 2 * DV_GLA)

    ts_xor = (lax.broadcasted_iota(jnp.int32, (C, C), 0)
              ^ lax.broadcasted_iota(jnp.int32, (C, C), 1))
    tok = lax.broadcasted_iota(jnp.int32, (C, LANES), 0)
    a0 = jnp.zeros((C, C), F32)
    a1 = jnp.zeros((C, C), F32)
    m = GLA_DIAG
    while m < C:
        P = C // (2 * m)
        bmid = _bcast_row(b.reshape(P, 2 * m, LANES), m).reshape(C, LANES)
        late = (tok & m) != 0
        qt = q * jnp.exp(jnp.where(late, b - bmid, neg_inf))
        kt = (k * jnp.exp(jnp.where(late, neg_inf, bmid - b))).astype(BF16)
        split_here = (ts_xor // m) == 1
        x0 = _dot_nt(jnp.where(head0, qt, 0.0).astype(BF16), kt)
        x1 = _dot_nt(jnp.where(head0, 0.0, qt).astype(BF16), kt)
        a0 = jnp.where(split_here, x0, a0)
        a1 = jnp.where(split_here, x1, a1)
        m *= 2
    o_intra = jnp.concatenate([_dot(a0.astype(BF16), vb[:, :DV_GLA]),
                               _dot(a1.astype(BF16), vb[:, DV_GLA:])], axis=1)

    st = st_ref[...]
    stb = st.astype(BF16)
    qe = q * jnp.exp(b)
    o_inter = jnp.concatenate(
        [_dot_nt(jnp.where(head0, qe, 0.0).astype(BF16), stb[:DV_GLA]),
         _dot_nt(jnp.where(head0, 0.0, qe).astype(BF16), stb[DV_GLA:])], axis=1)
    o = o + o_intra + o_inter

    b_last = b[C - 1:C, :]
    kd = (k * jnp.exp(b_last - b)).astype(BF16)
    kv0 = _dot(vb[:, :DV_GLA].T, kd)
    kv1 = _dot(vb[:, DV_GLA:].T, kd)
    head0_s = lax.broadcasted_iota(jnp.int32, (DV_GLA, LANES), 1) < DK_GLA
    dec = jnp.exp(b_last)
    st_new = jnp.concatenate([st[:DV_GLA] * dec + jnp.where(head0_s, kv0, 0.0),
                              st[DV_GLA:] * dec + jnp.where(head0_s, 0.0, kv1)], axis=0)
    st_ref[...] = st_new

    gate = gate_ref[...]
    gn = gn_ref[...]
    outs = []
    for hh in range(2):
        oh = o[:, hh * DV_GLA:(hh + 1) * DV_GLA]
        gh = gate[:, hh * DV_GLA:(hh + 1) * DV_GLA]
        r = lax.rsqrt(jnp.mean(oh * oh, axis=-1, keepdims=True) + EPS)
        outs.append(((oh * r) * gn) * (gh * (1.0 / (1.0 + jnp.exp(-gh)))))
    o_ref[...] = jnp.concatenate(outs, axis=1).astype(o_ref.dtype)

    @pl.when(c == pl.num_programs(2) - 1)
    def _():
        s0 = st_new[:DV_GLA].T
        s1 = st_new[DV_GLA:].T
        s_ref[0, 0] = s0[:DK_GLA]
        s_ref[0, 1] = s1[DK_GLA:]


def _gla_prompt(qg, kg, la, vg, gg, wts, batch, seq):
    C = GLA_C
    assert seq % C == 0
    nc = seq // C
    npair = H_GLA // 2
    qk_spec = pl.BlockSpec((C, LANES), lambda b, p, c: (b * nc + c, p))
    v_spec = pl.BlockSpec((C, 2 * DV_GLA), lambda b, p, c: (b * nc + c, p))
    return pl.pallas_call(
        _gla_prompt_kernel,
        grid=(batch, npair, nc),
        in_specs=[qk_spec, qk_spec, qk_spec, v_spec, v_spec,
                  _const_spec((C, C)), _const_spec((LANES, 2 * DV_GLA)), _const_spec((1, DV_GLA))],
        out_specs=[v_spec, pl.BlockSpec((1, 2, DK_GLA, DV_GLA), lambda b, p, c: (b, p, 0, 0))],
        out_shape=[jax.ShapeDtypeStruct((batch * seq, GLA_V_W), BF16),
                   jax.ShapeDtypeStruct((batch, H_GLA, DK_GLA, DV_GLA), F32)],
        scratch_shapes=[pltpu.VMEM((2 * DV_GLA, LANES), F32)],
        compiler_params=_cparams(("parallel", "parallel", "arbitrary")),
        name="gla_prompt",
    )(qg, kg, la, vg, gg, wts['tri'], wts['ind'], wts['gla_g'])


def _gla_step_kernel(q_ref, k_ref, la_ref, v_ref, gate_ref, s_ref, gn_ref, o_ref, so_ref):
    b = pl.program_id(0)
    nb = q_ref.shape[0]

    def column(ref):
        x = ref[...]
        pad = jnp.zeros((LANES - nb, x.shape[1]), F32)
        xt = jnp.concatenate([x, pad], axis=0).T
        lane = lax.broadcasted_iota(jnp.int32, xt.shape, 1)
        return jnp.sum(jnp.where(lane == b, xt, 0.0), axis=1, keepdims=True)

    qc = column(q_ref)
    kc = column(k_ref)
    ac = jnp.exp(column(la_ref))
    v = v_ref[0]
    gate = gate_ref[0]
    gn = gn_ref[...]
    outs = []
    for h in range(H_GLA):
        sl = slice(h * DK_GLA, (h + 1) * DK_GLA)
        vs = slice(h * DV_GLA, (h + 1) * DV_GLA)
        s_new = s_ref[0, h] * ac[sl] + kc[sl] * v[:, vs]
        so_ref[0, h] = s_new
        oh = jnp.sum(qc[sl] * s_new, axis=0, keepdims=True)
        r = lax.rsqrt(jnp.mean(oh * oh, axis=-1, keepdims=True) + EPS)
        gh = gate[:, vs]
        outs.append(((oh * r) * gn) * (gh * (1.0 / (1.0 + jnp.exp(-gh)))))
    o_ref[0] = jnp.concatenate(outs, axis=1)


def _gla_step(qg, kg, la, vg, gg, state, wts):
    nb = qg.shape[0]
    assert nb <= LANES
    full = _const_spec((nb, GLA_QK_W))
    row3 = pl.BlockSpec((1, 1, GLA_V_W), lambda b: (b, 0, 0))
    st_spec = pl.BlockSpec((1, H_GLA, DK_GLA, DV_GLA), lambda b: (b, 0, 0, 0))
    o, s = pl.pallas_call(
        _gla_step_kernel,
        grid=(nb,),
        in_specs=[full, full, full, row3, row3, st_spec, _const_spec((1, DV_GLA))],
        out_specs=[row3, st_spec],
        out_shape=[jax.ShapeDtypeStruct((nb, 1, GLA_V_W), F32),
                   jax.ShapeDtypeStruct(state.shape, F32)],
        compiler_params=_cparams(("parallel",)),
        name="gla_step",
    )(qg, kg, la, vg.reshape(nb, 1, GLA_V_W), gg.reshape(nb, 1, GLA_V_W), state, wts['gla_g'])
    return o.reshape(nb, GLA_V_W), s


def _diff_finish(o1, o2, lam, dn_g):
    o = o1 - lam * o2
    r = lax.rsqrt(jnp.mean(o * o, axis=-1, keepdims=True) + EPS)
    return ((o * r) * dn_g) * (1.0 - LAMBDA_INIT)


def _attn_prompt_kernel(q_ref, kt_ref, v_ref, lq1, lk1, lq2, lk2, dn_ref, o_ref,
                        m_ref, acc_ref, sa_ref, sb_ref):
    qi = pl.program_id(2)
    T = q_ref.shape[0]
    q = q_ref[...]
    lane = lax.broadcasted_iota(jnp.int32, (T, LANES), 1)
    zero = jnp.zeros_like(q)
    qs = (jnp.where(lane < DH_DIFF, q, zero), jnp.where(lane < DH_DIFF, zero, q))
    ones = jnp.ones((T, LANES), BF16)
    causal = (lax.broadcasted_iota(jnp.int32, (T, T), 1)
              <= lax.broadcasted_iota(jnp.int32, (T, T), 0))

    def scores(tile, s_ref):
        kt = kt_ref[0, :, pl.ds(pl.multiple_of(tile * T, T), T)]
        for c in range(2):
            s_ref[c] = _dot(qs[c], kt)

    def update(tile, s_ref, masked):
        v1 = jnp.concatenate([v_ref[pl.ds(pl.multiple_of(tile * T, T), T), :], ones], axis=1)
        for c in range(2):
            s = s_ref[c]
            if masked:
                s = jnp.where(causal, s, -jnp.inf)
            m_old = m_ref[c]
            m_new = jnp.maximum(m_old, jnp.max(s, axis=-1, keepdims=True))
            alpha = jnp.exp2(m_old - m_new)
            p = jnp.exp2(s - jnp.concatenate([m_new] * (T // LANES), axis=1))
            acc_ref[c] = (jnp.concatenate([alpha, alpha], axis=1) * acc_ref[c]
                          + _dot(p.astype(BF16), v1))
            m_ref[c] = m_new

    m_ref[...] = jnp.full_like(m_ref, -jnp.inf)
    acc_ref[...] = jnp.zeros_like(acc_ref)
    scores(0, sa_ref)

    def body(j, carry):
        t = 2 * j
        scores(t + 1, sb_ref)
        update(t, sa_ref, False)
        scores(t + 2, sa_ref)
        update(t + 1, sb_ref, False)
        return carry

    lax.fori_loop(0, lax.shift_right_logical(qi, 1), body, 0)

    @pl.when((qi & 1) == 0)
    def _():
        update(qi, sa_ref, True)

    @pl.when((qi & 1) == 1)
    def _():
        scores(qi, sb_ref)
        update(qi - 1, sa_ref, False)
        update(qi, sb_ref, True)

    lam = _lambda_full(lq1[...], lk1[...], lq2[...], lk2[...])
    on = [acc_ref[c][:, :DV_DIFF] / acc_ref[c][:, DV_DIFF:] for c in range(2)]
    o_ref[...] = _diff_finish(on[0], on[1], lam, dn_ref[...]).astype(o_ref.dtype)


def _attn_prompt(qd, kdt, vd, wts, batch, seq):
    T = ATT_T
    assert seq % T == 0 and (T & (T - 1)) == 0
    nq = seq // T
    q_spec = pl.BlockSpec((T, LANES), lambda b, h, i: (b * nq + i, h))
    kt_spec = pl.BlockSpec((1, LANES, seq), lambda b, h, i: (b, h, 0))
    v_spec = pl.BlockSpec((seq, LANES), lambda b, h, i: (b, h))
    lam_spec = _const_spec((1, DH_DIFF))
    return pl.pallas_call(
        _attn_prompt_kernel,
        grid=(batch, H_DIFF, nq),
        in_specs=[q_spec, kt_spec, v_spec, lam_spec, lam_spec, lam_spec, lam_spec,
                  _const_spec((1, DV_DIFF))],
        out_specs=q_spec,
        out_shape=jax.ShapeDtypeStruct((batch * seq, DIFF_V_W), BF16),
        scratch_shapes=[pltpu.VMEM((2, T, LANES), F32), pltpu.VMEM((2, T, 2 * DV_DIFF), F32),
                        pltpu.VMEM((2, T, T), F32), pltpu.VMEM((2, T, T), F32)],
        compiler_params=_cparams(("parallel", "parallel", "arbitrary")),
        name="attn_prompt",
    )(qd, kdt, vd, wts['lq1'], wts['lk1'], wts['lq2'], wts['lk2'], wts['dn_g'])


def _attn_decode_kernel(pt_ref, q_ref, kn_ref, vn_ref, lq1, lk1, lq2, lk2, dn_ref, *rest):
    np_ = DEC_PAGES
    k_refs = rest[:np_]
    v_refs = rest[np_:2 * np_]
    o_ref, qbd_ref, m_ref, l_ref, acc_ref = rest[2 * np_:]
    j = pl.program_id(1)
    R = 2 * H_DIFF
    rowi = lax.broadcasted_iota(jnp.int32, (R, DIFF_QK_W), 0)
    lanei = lax.broadcasted_iota(jnp.int32, (R, DIFF_QK_W), 1)

    @pl.when(j == 0)
    def _():
        qb = jnp.broadcast_to(q_ref[0], (R, DIFF_QK_W))
        qbd_ref[...] = jnp.where(lanei // DH_DIFF == rowi, qb, 0.0).astype(BF16)
        m_ref[...] = jnp.full_like(m_ref, NEG_BIG)
        l_ref[...] = jnp.zeros_like(l_ref)
        acc_ref[...] = jnp.zeros_like(acc_ref)

    qbd = qbd_ref[...]
    s = jnp.concatenate([_dot(qbd, k_refs[i][0].astype(BF16)) for i in range(np_)], axis=1)
    m_old = m_ref[...]
    m_new = jnp.maximum(m_old, jnp.max(s, axis=-1, keepdims=True))
    alpha = jnp.exp2(m_old - m_new)
    p = jnp.exp2(s - m_new)
    l_new = alpha * l_ref[...] + jnp.sum(p, axis=-1, keepdims=True)
    pb = p.astype(BF16)
    acc_ref[...] = alpha * acc_ref[...]
    for i in range(np_):
        pi = pb[:, i * LANES:(i + 1) * LANES]
        for h in range(H_DIFF):
            vh = v_refs[i].at[0][pl.ds(h, LANES, stride=H_DIFF), :]
            acc_ref[:, h * DV_DIFF:(h + 1) * DV_DIFF] += _dot(pi, vh.astype(BF16))
    acc_new = acc_ref[...]
    m_ref[...] = m_new
    l_ref[...] = l_new

    @pl.when(j == pl.num_programs(1) - 1)
    def _():
        s_n = jnp.sum(qbd.astype(F32) * kn_ref[0], axis=-1, keepdims=True)
        m_f = jnp.maximum(m_new, s_n)
        a_f = jnp.exp2(m_new - m_f)
        p_n = jnp.exp2(s_n - m_f)
        l_f = a_f * l_new + p_n
        acc_f = a_f * acc_new + p_n * vn_ref[0]
        lam = _lambda_full(lq1[...], lk1[...], lq2[...], lk2[...])
        coef = jnp.where((rowi & 1) == 0, 1.0, -lam)
        w = jnp.where(lanei // DV_DIFF == rowi // 2, (acc_f / l_f) * coef, 0.0)
        o = jnp.sum(w, axis=0, keepdims=True)
        dn = dn_ref[...]
        outs = []
        for h in range(H_DIFF):
            oh = o[:, h * DV_DIFF:(h + 1) * DV_DIFF]
            outs.append(_diff_finish(oh, jnp.zeros_like(oh), lam, dn))
        o_ref[0] = jnp.concatenate(outs, axis=1)


def _attn_decode(qd, kd_new, vd_new, cache_kt, cache_vr, page_table, wts):
    nb, n_pages = page_table.shape
    page = cache_kt.shape[2]
    assert n_pages % DEC_PAGES == 0 and page == LANES
    nj = n_pages // DEC_PAGES
    row3 = pl.BlockSpec((1, 1, DIFF_QK_W), lambda b, j, pt: (b, 0, 0))

    def page_spec(i):
        return pl.BlockSpec((1, DIFF_QK_W, LANES),
                            lambda b, j, pt: (pt[b * n_pages + j * DEC_PAGES + i], 0, 0))

    lam_spec = pl.BlockSpec((1, DH_DIFF), lambda b, j, pt: (0, 0))
    grid_spec = pltpu.PrefetchScalarGridSpec(
        num_scalar_prefetch=1,
        grid=(nb, nj),
        in_specs=[row3, row3, row3, lam_spec, lam_spec, lam_spec, lam_spec,
                  pl.BlockSpec((1, DV_DIFF), lambda b, j, pt: (0, 0))]
                 + [page_spec(i) for i in range(DEC_PAGES)] * 2,
        out_specs=row3,
        scratch_shapes=[pltpu.VMEM((2 * H_DIFF, DIFF_QK_W), BF16),
                        pltpu.VMEM((2 * H_DIFF, 1), F32), pltpu.VMEM((2 * H_DIFF, 1), F32),
                        pltpu.VMEM((2 * H_DIFF, DIFF_V_W), F32)],
    )
    o = pl.pallas_call(
        _attn_decode_kernel,
        grid_spec=grid_spec,
        out_shape=jax.ShapeDtypeStruct((nb, 1, DIFF_V_W), F32),
        compiler_params=_cparams(("parallel", "arbitrary")),
        name="attn_decode",
    )(page_table.reshape(-1), qd.astype(F32).reshape(nb, 1, DIFF_QK_W),
      kd_new.reshape(nb, 1, DIFF_QK_W), vd_new.reshape(nb, 1, DIFF_V_W),
      wts['lq1'], wts['lk1'], wts['lq2'], wts['lk2'], wts['dn_g'],
      *([cache_kt] * DEC_PAGES), *([cache_vr] * DEC_PAGES))
    return o.reshape(nb, DIFF_V_W)


def _out_ffn_kernel(x_ref, mg_ref, md_ref, wo_ref, g_ref, wup_ref, wdn_ref, y_ref, *, fc):
    half = GLA_V_W
    x1 = (x_ref[...] + _dot(mg_ref[...].astype(BF16), wo_ref[0:half, :])
          + _dot(md_ref[...].astype(BF16), wo_ref[half:, :]))
    r = lax.rsqrt(jnp.mean(x1 * x1, axis=-1, keepdims=True) + EPS)
    h2 = ((x1 * r) * g_ref[...]).astype(BF16)
    y_ref[...] = x1
    for f in range(D_FF // fc):
        u = jnp.maximum(_dot(h2, wup_ref[:, f * fc:(f + 1) * fc]), 0.0)
        y_ref[...] += _dot((u * u).astype(BF16), wdn_ref[f * fc:(f + 1) * fc, :])


def _out_ffn(x, mix_g, mix_d, wts, tm, fc=512):
    n = x.shape[0]
    assert n % tm == 0
    row = lambda w: pl.BlockSpec((tm, w), lambda i: (i, 0))
    consts = [wts['w_out'], wts['g_ffn'], wts['w_up'], wts['w_down']]
    const_specs = [_resident_spec(c.shape) for c in consts]
    return pl.pallas_call(
        functools.partial(_out_ffn_kernel, fc=fc),
        grid=(n // tm,),
        in_specs=[row(D_MODEL), row(GLA_V_W), row(DIFF_V_W)] + const_specs,
        out_specs=row(D_MODEL),
        out_shape=jax.ShapeDtypeStruct((n, D_MODEL), F32),
        compiler_params=_cparams(("parallel",)),
        name="out_ffn",
    )(x, mix_g, mix_d, *consts)


def _prep_weights(norm_mix_g, w_in, w_gk2, b_gk, gla_norm_g, q_norm_g, k_norm_g,
                  lambda_q1, lambda_k1, lambda_q2, lambda_k2, diff_norm_g, w_out,
                  norm_ffn_g, w_up, w_down):
    w = w_in[0]
    blk = np.arange(2 * LANES) // DH_DIFF
    ind = (np.arange(LANES)[:, None] // DK_GLA) == (np.arange(2 * DV_GLA)[None, :] // DV_GLA)
    return {
        'g_mix': norm_mix_g[0][None, :],
        'w_gla': w[:, :OFF_GLR].astype(BF16),
        'w_glr': jnp.pad(w[:, OFF_GLR:OFF_DIFF], ((0, 0), (0, LANES - GK_RANK))).astype(BF16),
        'w_gk2': jnp.pad(w_gk2[0], ((0, LANES - GK_RANK), (0, 0))).astype(BF16),
        'b_gk': b_gk[0][None, :],
        'w_diff': w[:, OFF_DIFF:].astype(BF16),
        'qn_g': jnp.tile(q_norm_g[0], 2 * H_DIFF)[None, :],
        'kn_g': jnp.tile(k_norm_g[0], 2 * H_DIFF)[None, :],
        'ones_blk': jnp.asarray(blk[:, None] == blk[None, :], BF16),
        'tri': jnp.asarray(np.tril(np.ones((GLA_C, GLA_C))), BF16),
        'ind': jnp.asarray(ind, BF16),
        'gla_g': gla_norm_g[0][None, :],
        'lq1': lambda_q1[0][None, :], 'lk1': lambda_k1[0][None, :],
        'lq2': lambda_q2[0][None, :], 'lk2': lambda_k2[0][None, :],
        'dn_g': diff_norm_g[0][None, :],
        'w_out': w_out[0].astype(BF16),
        'g_ffn': norm_ffn_g[0][None, :],
        'w_up': w_up[0].astype(BF16),
        'w_down': w_down[0].astype(BF16),
    }


def kernel(x_prompt, x_sample, cache_k, cache_v, page_table, state_gla, norm_mix_g, w_in, w_gk2, b_gk, gla_norm_g, q_norm_g, k_norm_g, lambda_q1, lambda_k1, lambda_q2, lambda_k2, diff_norm_g, w_out, norm_ffn_g, w_up, w_down):
    assert w_in.shape[0] == 1, "single-layer trunk"
    B, T, D = x_prompt.shape
    Bd, Td, _ = x_sample.shape
    assert Td == 1
    wts = _prep_weights(norm_mix_g, w_in, w_gk2, b_gk, gla_norm_g, q_norm_g, k_norm_g,
                        lambda_q1, lambda_k1, lambda_q2, lambda_k2, diff_norm_g, w_out,
                        norm_ffn_g, w_up, w_down)

    xp = x_prompt.reshape(B * T, D)
    qg, kg, vg, gg, la, qd, kdf_t, kdb_t, vdf_r, vdb = _in_proj(xp, wts, tm=TOK_TILE, seq=T)
    mix_g, s_prompt = _gla_prompt(qg, kg, la, vg, gg, wts, B, T)
    mix_d = _attn_prompt(qd, kdb_t, vdb, wts, B, T)
    y_prompt = _out_ffn(xp, mix_g, mix_d, wts, tm=TOK_TILE)
    k_prompt = jnp.transpose(kdf_t.reshape(1, B, H_DIFF, 2, DH_DIFF, T), (0, 1, 5, 2, 3, 4))
    v_prompt = vdf_r.reshape(1, B, T, H_DIFF, DV_DIFF)

    xs = x_sample.reshape(Bd, D)
    qg_s, kg_s, vg_s, gg_s, la_s, qd_s, kdf_s, _, vdf_s, _ = _in_proj(xs, wts, tm=Bd)
    mixg_s, s_sample = _gla_step(qg_s, kg_s, la_s, vg_s, gg_s, state_gla[0], wts)
    n_pool, page = cache_k.shape[1], cache_k.shape[2]
    cache_kt = jnp.transpose(cache_k[0], (0, 2, 3, 4, 1)).reshape(n_pool, DIFF_QK_W, page)
    cache_vr = cache_v[0].reshape(n_pool, page * H_DIFF, DV_DIFF)
    mixd_s = _attn_decode(qd_s, kdf_s, vdf_s, cache_kt, cache_vr, page_table, wts)
    y_sample = _out_ffn(xs, mixg_s, mixd_s, wts, tm=Bd)

    return (y_prompt.reshape(B, T, D),
            y_sample.reshape(Bd, Td, D),
            k_prompt,
            v_prompt,
            s_prompt[None],
            kdf_s.reshape(1, Bd, Td, H_DIFF, 2, DH_DIFF),
            vdf_s.reshape(1, Bd, Td, H_DIFF, DV_DIFF),
            s_sample[None])
```

```python
import functools
import math

import jax
import jax.numpy as jnp
import numpy as np
from jax import lax
from jax.experimental import pallas as pl
from jax.experimental.pallas import tpu as pltpu

F32 = jnp.float32
BF16 = jnp.bfloat16

LANES = 128
SUBLANES = 8
VMEM_LIMIT = 56 * 1024 * 1024

D_MODEL = 1024
H_GLA = 4
DK_GLA = 64
DV_GLA = 128
GK_RANK = 16
GK_NORMALIZER = 16.0
H_DIFF = 4
DH_DIFF = 64
DV_DIFF = 128
D_FF = 4 * D_MODEL
EPS = 1e-6
LAMBDA_INIT = 0.8 - 0.6 * math.exp(-0.3 * 0)

GLA_QK_W = H_GLA * DK_GLA
GLA_V_W = H_GLA * DV_GLA
DIFF_QK_W = H_DIFF * 2 * DH_DIFF
DIFF_V_W = H_DIFF * DV_DIFF
OFF_GLR = 2 * GLA_QK_W + 2 * GLA_V_W
OFF_DIFF = OFF_GLR + GK_RANK

LOG2E = 1.4426950408889634
NEG_BIG = -1e30

GLA_C = 256
GLA_DIAG = 8
ATT_T = 512
TOK_TILE = 512


def _cparams(sem):
    return pltpu.CompilerParams(dimension_semantics=sem, vmem_limit_bytes=VMEM_LIMIT)


def _const_spec(shape):
    nd = len(shape)
    return pl.BlockSpec(shape, lambda *_: (0,) * nd)


def _resident_spec(shape):
    nd = len(shape)
    return pl.BlockSpec(shape, lambda *_: (0,) * nd, pipeline_mode=pl.Buffered(1))


def _dot(a, b):
    return jnp.dot(a, b, preferred_element_type=F32)


def _dot_nt(a, b):
    return lax.dot_general(a, b, (((1,), (1,)), ((), ())), preferred_element_type=F32)


def _lambda_full(lq1, lk1, lq2, lk2):
    a = jnp.sum(lq1 * lk1, axis=-1, keepdims=True)
    b = jnp.sum(lq2 * lk2, axis=-1, keepdims=True)
    return jnp.exp(a) - jnp.exp(b) + LAMBDA_INIT


def _group_rms(z, ones_blk, group):
    outs = []
    w = ones_blk.shape[0]
    for c in range(z.shape[1] // w):
        zc = z[:, c * w:(c + 1) * w]
        outs.append(_dot((zc * zc).astype(BF16), ones_blk))
    ss = jnp.concatenate(outs, axis=1)
    return lax.rsqrt(ss * (1.0 / group) + EPS)


def _in_proj_kernel(x_ref, g_ref, wgla_ref, wglr_ref, wgk2_ref, bgk_ref, wdiff_ref,
                    qng_ref, kng_ref, ones_ref,
                    qg_ref, kg_ref, vg_ref, gg_ref, la_ref,
                    qd_ref, kdf_ref, kdb_ref, vdf_ref, vdb_ref, *, cache_layout):
    x = x_ref[...]
    r = lax.rsqrt(jnp.mean(x * x, axis=-1, keepdims=True) + EPS)
    h = ((x * r) * g_ref[...]).astype(BF16)

    qg_ref[...] = _dot(h, wgla_ref[:, 0:GLA_QK_W]) * (DK_GLA ** -0.5)
    kg_ref[...] = _dot(h, wgla_ref[:, GLA_QK_W:2 * GLA_QK_W])
    vg_ref[...] = _dot(h, wgla_ref[:, 2 * GLA_QK_W:2 * GLA_QK_W + GLA_V_W])
    gg_ref[...] = _dot(h, wgla_ref[:, 2 * GLA_QK_W + GLA_V_W:])

    glr = _dot(h, wglr_ref[...])
    gl = _dot(glr.astype(BF16), wgk2_ref[...]) + bgk_ref[...]
    log_sig = jnp.minimum(gl, 0.0) - jnp.log1p(jnp.exp(-jnp.abs(gl)))
    la_ref[...] = log_sig / GK_NORMALIZER

    ones_blk = ones_ref[...]
    dq = _dot(h, wdiff_ref[:, 0:DIFF_QK_W])
    qn = (dq * _group_rms(dq, ones_blk, DH_DIFF)) * qng_ref[...]
    qd_ref[...] = (qn * (DH_DIFF ** -0.5 * LOG2E)).astype(BF16)
    dk = _dot(h, wdiff_ref[:, DIFF_QK_W:2 * DIFF_QK_W])
    kn = (dk * _group_rms(dk, ones_blk, DH_DIFF)) * kng_ref[...]
    dv = _dot(h, wdiff_ref[:, 2 * DIFF_QK_W:])
    vdb_ref[...] = dv.astype(BF16)
    if cache_layout:
        knt = kn.T
        kdf_ref[0] = knt
        kdb_ref[0] = knt.astype(BF16)
        tm = dv.shape[0]
        for hh in range(H_DIFF):
            vdf_ref[pl.ds(hh, tm, stride=H_DIFF), :] = dv[:, hh * DV_DIFF:(hh + 1) * DV_DIFF]
    else:
        kdf_ref[...] = kn
        kdb_ref[...] = kn.astype(BF16)
        vdf_ref[...] = dv


def _in_proj(x, wts, tm, seq=None):
    n = x.shape[0]
    assert n % tm == 0
    row = lambda w: pl.BlockSpec((tm, w), lambda i: (i, 0))
    out_w = [(GLA_QK_W, F32), (GLA_QK_W, F32), (GLA_V_W, F32), (GLA_V_W, F32), (GLA_QK_W, F32),
             (DIFF_QK_W, BF16), (DIFF_QK_W, F32), (DIFF_QK_W, BF16), (DIFF_V_W, F32), (DIFF_V_W, BF16)]
    out_specs = [row(w) for w, _ in out_w]
    out_shape = [jax.ShapeDtypeStruct((n, w), dt) for w, dt in out_w]
    if seq is not None:
        assert seq % tm == 0 and n % seq == 0
        per = seq // tm
        kt_spec = pl.BlockSpec((1, DIFF_QK_W, tm), lambda i: (i // per, 0, i % per))
        out_specs[6] = out_specs[7] = kt_spec
        out_shape[6] = jax.ShapeDtypeStruct((n // seq, DIFF_QK_W, seq), F32)
        out_shape[7] = jax.ShapeDtypeStruct((n // seq, DIFF_QK_W, seq), BF16)
        out_specs[8] = pl.BlockSpec((tm * H_DIFF, DV_DIFF), lambda i: (i, 0))
        out_shape[8] = jax.ShapeDtypeStruct((n * H_DIFF, DV_DIFF), F32)
    consts = [wts['g_mix'], wts['w_gla'], wts['w_glr'], wts['w_gk2'], wts['b_gk'], wts['w_diff'],
              wts['qn_g'], wts['kn_g'], wts['ones_blk']]
    return pl.pallas_call(
        functools.partial(_in_proj_kernel, cache_layout=seq is not None),
        grid=(n // tm,),
        in_specs=[row(D_MODEL)] + [_resident_spec(c.shape) for c in consts],
        out_specs=out_specs,
        out_shape=out_shape,
        compiler_params=_cparams(("parallel",)),
        name="in_proj",
    )(x, *consts)


def _bcast_row(x3, j):
    return jnp.broadcast_to(x3[:, j:j + 1, :], x3.shape)


def _gla_prompt_kernel(q_ref, k_ref, la_ref, v_ref, gate_ref, tri_ref, ind_ref, gn_ref,
                       o_ref, s_ref, st_ref):
    c = pl.program_id(2)
    C = q_ref.shape[0]
    neg_inf = -jnp.inf

    @pl.when(c == 0)
    def _():
        st_ref[...] = jnp.zeros_like(st_ref)

    la = la_ref[...]
    tri = tri_ref[...]
    t1 = la.astype(BF16)
    r1 = la - t1.astype(F32)
    t2 = r1.astype(BF16)
    t3 = (r1 - t2.astype(F32)).astype(BF16)
    b = _dot(tri, t1) + _dot(tri, t2) + _dot(tri, t3)

    q = q_ref[...]
    k = k_ref[...]
    v = v_ref[...]
    vb = v.astype(BF16)
    lane = lax.broadcasted_iota(jnp.int32, (C, LANES), 1)
    head0 = lane < DK_GLA

    G = C // GLA_DIAG
    b3 = b.reshape(G, GLA_DIAG, LANES)
    q3 = q.reshape(G, GLA_DIAG, LANES)
    k3 = k.reshape(G, GLA_DIAG, LANES)
    v3 = v.reshape(G, GLA_DIAG, 2 * DV_GLA)
    sub = lax.broadcasted_iota(jnp.int32, (G, GLA_DIAG, LANES), 1)
    ind = ind_ref[...]
    o = jnp.zeros((C, 2 * DV_GLA), F32)
    for j in range(GLA_DIAG):
        e = jnp.exp(jnp.where(sub >= j, b3 - _bcast_row(b3, j), neg_inf))
        pj = (q3 * _bcast_row(k3, j) * e).reshape(C, LANES).astype(BF16)
        rj = _dot(pj, ind)
        o = o + rj * _bcast_row(v3, j).reshape(C, 2 * DV_GLA)

    ts_xor = (lax.broadcasted_iota(jnp.int32, (C, C), 0)
              ^ lax.broadcasted_iota(jnp.int32, (C, C), 1))
    tok = lax.broadcasted_iota(jnp.int32, (C, LANES), 0)
    a0 = jnp.zeros((C, C), F32)
    a1 = jnp.zeros((C, C), F32)
    m = GLA_DIAG
    while m < C:
        P = C // (2 * m)
        bmid = _bcast_row(b.reshape(P, 2 * m, LANES), m).reshape(C, LANES)
        late = (tok & m) != 0
        qt = q * jnp.exp(jnp.where(late, b - bmid, neg_inf))
        kt = (k * jnp.exp(jnp.where(late, neg_inf, bmid - b))).astype(BF16)
        split_here = (ts_xor // m) == 1
        x0 = _dot_nt(jnp.where(head0, qt, 0.0).astype(BF16), kt)
        x1 = _dot_nt(jnp.where(head0, 0.0, qt).astype(BF16), kt)
        a0 = jnp.where(split_here, x0, a0)
        a1 = jnp.where(split_here, x1, a1)
        m *= 2
    o_intra = jnp.concatenate([_dot(a0.astype(BF16), vb[:, :DV_GLA]),
                               _dot(a1.astype(BF16), vb[:, DV_GLA:])], axis=1)

    st = st_ref[...]
    stb = st.astype(BF16)
    qe = q * jnp.exp(b)
    o_inter = jnp.concatenate(
        [_dot_nt(jnp.where(head0, qe, 0.0).astype(BF16), stb[:DV_GLA]),
         _dot_nt(jnp.where(head0, 0.0, qe).astype(BF16), stb[DV_GLA:])], axis=1)
    o = o + o_intra + o_inter

    b_last = b[C - 1:C, :]
    kd = (k * jnp.exp(b_last - b)).astype(BF16)
    kv0 = _dot(vb[:, :DV_GLA].T, kd)
    kv1 = _dot(vb[:, DV_GLA:].T, kd)
    head0_s = lax.broadcasted_iota(jnp.int32, (DV_GLA, LANES), 1) < DK_GLA
    dec = jnp.exp(b_last)
    st_new = jnp.concatenate([st[:DV_GLA] * dec + jnp.where(head0_s, kv0, 0.0),
                              st[DV_GLA:] * dec + jnp.where(head0_s, 0.0, kv1)], axis=0)
    st_ref[...] = st_new

    gate = gate_ref[...]
    gn = gn_ref[...]
    outs = []
    for hh in range(2):
        oh = o[:, hh * DV_GLA:(hh + 1) * DV_GLA]
        gh = gate[:, hh * DV_GLA:(hh + 1) * DV_GLA]
        r = lax.rsqrt(jnp.mean(oh * oh, axis=-1, keepdims=True) + EPS)
        outs.append(((oh * r) * gn) * (gh * (1.0 / (1.0 + jnp.exp(-gh)))))
    o_ref[...] = jnp.concatenate(outs, axis=1).astype(o_ref.dtype)

    @pl.when(c == pl.num_programs(2) - 1)
    def _():
        s0 = st_new[:DV_GLA].T
        s1 = st_new[DV_GLA:].T
        s_ref[0, 0] = s0[:DK_GLA]
        s_ref[0, 1] = s1[DK_GLA:]


def _gla_prompt(qg, kg, la, vg, gg, wts, batch, seq):
    C = GLA_C
    assert seq % C == 0
    nc = seq // C
    npair = H_GLA // 2
    qk_spec = pl.BlockSpec((C, LANES), lambda b, p, c: (b * nc + c, p))
    v_spec = pl.BlockSpec((C, 2 * DV_GLA), lambda b, p, c: (b * nc + c, p))
    return pl.pallas_call(
        _gla_prompt_kernel,
        grid=(batch, npair, nc),
        in_specs=[qk_spec, qk_spec, qk_spec, v_spec, v_spec,
                  _const_spec((C, C)), _const_spec((LANES, 2 * DV_GLA)), _const_spec((1, DV_GLA))],
        out_specs=[v_spec, pl.BlockSpec((1, 2, DK_GLA, DV_GLA), lambda b, p, c: (b, p, 0, 0))],
        out_shape=[jax.ShapeDtypeStruct((batch * seq, GLA_V_W), BF16),
                   jax.ShapeDtypeStruct((batch, H_GLA, DK_GLA, DV_GLA), F32)],
        scratch_shapes=[pltpu.VMEM((2 * DV_GLA, LANES), F32)],
        compiler_params=_cparams(("parallel", "parallel", "arbitrary")),
        name="gla_prompt",
    )(qg, kg, la, vg, gg, wts['tri'], wts['ind'], wts['gla_g'])


def _gla_step_kernel(q_ref, k_ref, la_ref, v_ref, gate_ref, s_ref, gn_ref, o_ref, so_ref):
    b = pl.program_id(0)
    nb = q_ref.shape[0]

    def column(ref):
        x = ref[...]
        pad = jnp.zeros((LANES - nb, x.shape[1]), F32)
        xt = jnp.concatenate([x, pad], axis=0).T
        lane = lax.broadcasted_iota(jnp.int32, xt.shape, 1)
        return jnp.sum(jnp.where(lane == b, xt, 0.0), axis=1, keepdims=True)

    qc = column(q_ref)
    kc = column(k_ref)
    ac = jnp.exp(column(la_ref))
    v = v_ref[0]
    gate = gate_ref[0]
    gn = gn_ref[...]
    outs = []
    for h in range(H_GLA):
        sl = slice(h * DK_GLA, (h + 1) * DK_GLA)
        vs = slice(h * DV_GLA, (h + 1) * DV_GLA)
        s_new = s_ref[0, h] * ac[sl] + kc[sl] * v[:, vs]
        so_ref[0, h] = s_new
        oh = jnp.sum(qc[sl] * s_new, axis=0, keepdims=True)
        r = lax.rsqrt(jnp.mean(oh * oh, axis=-1, keepdims=True) + EPS)
        gh = gate[:, vs]
        outs.append(((oh * r) * gn) * (gh * (1.0 / (1.0 + jnp.exp(-gh)))))
    o_ref[0] = jnp.concatenate(outs, axis=1)


def _gla_step(qg, kg, la, vg, gg, state, wts):
    nb = qg.shape[0]
    assert nb <= LANES
    full = _const_spec((nb, GLA_QK_W))
    row3 = pl.BlockSpec((1, 1, GLA_V_W), lambda b: (b, 0, 0))
    st_spec = pl.BlockSpec((1, H_GLA, DK_GLA, DV_GLA), lambda b: (b, 0, 0, 0))
    o, s = pl.pallas_call(
        _gla_step_kernel,
        grid=(nb,),
        in_specs=[full, full, full, row3, row3, st_spec, _const_spec((1, DV_GLA))],
        out_specs=[row3, st_spec],
        out_shape=[jax.ShapeDtypeStruct((nb, 1, GLA_V_W), F32),
                   jax.ShapeDtypeStruct(state.shape, F32)],
        compiler_params=_cparams(("parallel",)),
        name="gla_step",
    )(qg, kg, la, vg.reshape(nb, 1, GLA_V_W), gg.reshape(nb, 1, GLA_V_W), state, wts['gla_g'])
    return o.reshape(nb, GLA_V_W), s


def _diff_finish(o1, o2, lam, dn_g):
    o = o1 - lam * o2
    r = lax.rsqrt(jnp.mean(o * o, axis=-1, keepdims=True) + EPS)
    return ((o * r) * dn_g) * (1.0 - LAMBDA_INIT)


def _decode_step(j, nj, lam, q_ref, kn_ref, vn_ref, dn_ref, k_refs, v_refs, o_ref,
                 qbd_ref, m_ref, l_ref, acc_ref):
    R = 2 * H_DIFF
    rowi = lax.broadcasted_iota(jnp.int32, (R, DIFF_QK_W), 0)
    lanei = lax.broadcasted_iota(jnp.int32, (R, DIFF_QK_W), 1)

    @pl.when(j == 0)
    def _():
        qb = jnp.broadcast_to(q_ref[0], (R, DIFF_QK_W))
        qbd_ref[...] = jnp.where(lanei // DH_DIFF == rowi, qb, 0.0).astype(BF16)
        m_ref[...] = jnp.full_like(m_ref, NEG_BIG)
        l_ref[...] = jnp.zeros_like(l_ref)
        acc_ref[...] = jnp.zeros_like(acc_ref)

    qbd = qbd_ref[...]
    s = jnp.concatenate([_dot(qbd, kr[0].astype(BF16)) for kr in k_refs], axis=1)
    m_old = m_ref[...]
    m_new = jnp.maximum(m_old, jnp.max(s, axis=-1, keepdims=True))
    alpha = jnp.exp2(m_old - m_new)
    p = jnp.exp2(s - m_new)
    l_new = alpha * l_ref[...] + jnp.sum(p, axis=-1, keepdims=True)
    pb = p.astype(BF16)
    acc_ref[...] = alpha * acc_ref[...]
    for i, vr in enumerate(v_refs):
        pi = pb[:, i * LANES:(i + 1) * LANES]
        for h in range(H_DIFF):
            vh = vr.at[0][pl.ds(h, LANES, stride=H_DIFF), :]
            acc_ref[:, h * DV_DIFF:(h + 1) * DV_DIFF] += _dot(pi, vh.astype(BF16))
    acc_new = acc_ref[...]
    m_ref[...] = m_new
    l_ref[...] = l_new

    @pl.when(j == nj - 1)
    def _():
        s_n = jnp.sum(qbd.astype(F32) * kn_ref[0], axis=-1, keepdims=True)
        m_f = jnp.maximum(m_new, s_n)
        a_f = jnp.exp2(m_new - m_f)
        p_n = jnp.exp2(s_n - m_f)
        l_f = a_f * l_new + p_n
        acc_f = a_f * acc_new + p_n * vn_ref[0]
        coef = jnp.where((rowi & 1) == 0, 1.0, -lam)
        w = jnp.where(lanei // DV_DIFF == rowi // 2, (acc_f / l_f) * coef, 0.0)
        o = jnp.sum(w, axis=0, keepdims=True)
        dn = dn_ref[...]
        outs = []
        for h in range(H_DIFF):
            oh = o[:, h * DV_DIFF:(h + 1) * DV_DIFF]
            outs.append(_diff_finish(oh, jnp.zeros_like(oh), lam, dn))
        o_ref[0] = jnp.concatenate(outs, axis=1)


def _attention_kernel(pt_ref, q_ref, kt_ref, v_ref, lq1, lk1, lq2, lk2, dn_ref,
                      dq_ref, dkn_ref, dvn_ref, *rest, n_pages_step, n_chunks):
    del pt_ref
    k_refs = rest[:n_pages_step]
    v_refs = rest[n_pages_step:2 * n_pages_step]
    (o_ref, do_ref, m_ref, acc_ref, sa_ref, sb_ref,
     qbd_ref, dm_ref, dl_ref, dacc_ref) = rest[2 * n_pages_step:]
    qi = pl.program_id(2)
    lam = _lambda_full(lq1[...], lk1[...], lq2[...], lk2[...])

    step = (pl.program_id(0) * pl.num_programs(1) + pl.program_id(1)) * pl.num_programs(2) + qi
    _decode_step(lax.rem(step, n_chunks), n_chunks, lam, dq_ref, dkn_ref, dvn_ref, dn_ref,
                 k_refs, v_refs, do_ref, qbd_ref, dm_ref, dl_ref, dacc_ref)

    T = q_ref.shape[0]
    q = q_ref[...]
    lane = lax.broadcasted_iota(jnp.int32, (T, LANES), 1)
    zero = jnp.zeros_like(q)
    qs = (jnp.where(lane < DH_DIFF, q, zero), jnp.where(lane < DH_DIFF, zero, q))
    ones = jnp.ones((T, LANES), BF16)
    causal = (lax.broadcasted_iota(jnp.int32, (T, T), 1)
              <= lax.broadcasted_iota(jnp.int32, (T, T), 0))

    def scores(tile, s_ref):
        kt = kt_ref[0, :, pl.ds(pl.multiple_of(tile * T, T), T)]
        for c in range(2):
            s_ref[c] = _dot(qs[c], kt)

    def update(tile, s_ref, masked):
        v1 = jnp.concatenate([v_ref[pl.ds(pl.multiple_of(tile * T, T), T), :], ones], axis=1)
        for c in range(2):
            s = s_ref[c]
            if masked:
                s = jnp.where(causal, s, -jnp.inf)
            m_old = m_ref[c]
            m_new = jnp.maximum(m_old, jnp.max(s, axis=-1, keepdims=True))
            alpha = jnp.exp2(m_old - m_new)
            p = jnp.exp2(s - jnp.concatenate([m_new] * (T // LANES), axis=1))
            acc_ref[c] = (jnp.concatenate([alpha, alpha], axis=1) * acc_ref[c]
                          + _dot(p.astype(BF16), v1))
            m_ref[c] = m_new

    m_ref[...] = jnp.full_like(m_ref, -jnp.inf)
    acc_ref[...] = jnp.zeros_like(acc_ref)
    scores(0, sa_ref)

    def body(j, carry):
        t = 2 * j
        scores(t + 1, sb_ref)
        update(t, sa_ref, False)
        scores(t + 2, sa_ref)
        update(t + 1, sb_ref, False)
        return carry

    lax.fori_loop(0, lax.shift_right_logical(qi, 1), body, 0)

    @pl.when((qi & 1) == 0)
    def _():
        update(qi, sa_ref, True)

    @pl.when((qi & 1) == 1)
    def _():
        scores(qi, sb_ref)
        update(qi - 1, sa_ref, False)
        update(qi, sb_ref, True)

    on = [acc_ref[c][:, :DV_DIFF] / acc_ref[c][:, DV_DIFF:] for c in range(2)]
    o_ref[...] = _diff_finish(on[0], on[1], lam, dn_ref[...]).astype(o_ref.dtype)


def _attention(qd, kdt, vd, qd_s, kd_s, vd_s, cache_kt, cache_vr, page_table, wts, batch, seq):
    T = ATT_T
    assert seq % T == 0 and (T & (T - 1)) == 0
    nq = seq // T
    nb, n_pages = page_table.shape
    assert cache_kt.shape[2] == LANES
    n_steps = batch * H_DIFF * nq
    assert (nb * n_pages) % n_steps == 0
    pps = (nb * n_pages) // n_steps
    assert n_pages % pps == 0
    n_chunks = n_pages // pps

    def step_of(b, h, i):
        return (b * H_DIFF + h) * nq + i

    q_spec = pl.BlockSpec((T, LANES), lambda b, h, i, pt: (b * nq + i, h))
    kt_spec = pl.BlockSpec((1, LANES, seq), lambda b, h, i, pt: (b, h, 0))
    v_spec = pl.BlockSpec((seq, LANES), lambda b, h, i, pt: (b, h))
    lam_spec = pl.BlockSpec((1, DH_DIFF), lambda b, h, i, pt: (0, 0))
    row3 = pl.BlockSpec((1, 1, DIFF_QK_W), lambda b, h, i, pt: (step_of(b, h, i) // n_chunks, 0, 0))

    def page_spec(k):
        return pl.BlockSpec((1, DIFF_QK_W, LANES),
                            lambda b, h, i, pt: (pt[step_of(b, h, i) * pps + k], 0, 0))

    grid_spec = pltpu.PrefetchScalarGridSpec(
        num_scalar_prefetch=1,
        grid=(batch, H_DIFF, nq),
        in_specs=[q_spec, kt_spec, v_spec, lam_spec, lam_spec, lam_spec, lam_spec,
                  pl.BlockSpec((1, DV_DIFF), lambda b, h, i, pt: (0, 0)), row3, row3, row3]
                 + [page_spec(k) for k in range(pps)] * 2,
        out_specs=[q_spec, row3],
        scratch_shapes=[pltpu.VMEM((2, T, LANES), F32), pltpu.VMEM((2, T, 2 * DV_DIFF), F32),
                        pltpu.VMEM((2, T, T), F32), pltpu.VMEM((2, T, T), F32),
                        pltpu.VMEM((2 * H_DIFF, DIFF_QK_W), BF16),
                        pltpu.VMEM((2 * H_DIFF, 1), F32), pltpu.VMEM((2 * H_DIFF, 1), F32),
                        pltpu.VMEM((2 * H_DIFF, DIFF_V_W), F32)],
    )
    o, o_s = pl.pallas_call(
        functools.partial(_attention_kernel, n_pages_step=pps, n_chunks=n_chunks),
        grid_spec=grid_spec,
        out_shape=[jax.ShapeDtypeStruct((batch * seq, DIFF_V_W), BF16),
                   jax.ShapeDtypeStruct((nb, 1, DIFF_V_W), F32)],
        compiler_params=_cparams(("arbitrary", "arbitrary", "arbitrary")),
        name="attention",
    )(page_table.reshape(-1), qd, kdt, vd,
      wts['lq1'], wts['lk1'], wts['lq2'], wts['lk2'], wts['dn_g'],
      qd_s.astype(F32).reshape(nb, 1, DIFF_QK_W), kd_s.reshape(nb, 1, DIFF_QK_W),
      vd_s.reshape(nb, 1, DIFF_V_W), *([cache_kt] * pps), *([cache_vr] * pps))
    return o, o_s.reshape(nb, DIFF_V_W)


def _out_ffn_kernel(x_ref, mg_ref, md_ref, wo_ref, g_ref, wup_ref, wdn_ref, y_ref, *, fc):
    half = GLA_V_W
    x1 = (x_ref[...] + _dot(mg_ref[...].astype(BF16), wo_ref[0:half, :])
          + _dot(md_ref[...].astype(BF16), wo_ref[half:, :]))
    r = lax.rsqrt(jnp.mean(x1 * x1, axis=-1, keepdims=True) + EPS)
    h2 = ((x1 * r) * g_ref[...]).astype(BF16)
    y_ref[...] = x1
    for f in range(D_FF // fc):
        u = jnp.maximum(_dot(h2, wup_ref[:, f * fc:(f + 1) * fc]), 0.0)
        y_ref[...] += _dot((u * u).astype(BF16), wdn_ref[f * fc:(f + 1) * fc, :])


def _out_ffn(x, mix_g, mix_d, wts, tm, fc=512):
    n = x.shape[0]
    assert n % tm == 0
    row = lambda w: pl.BlockSpec((tm, w), lambda i: (i, 0))
    consts = [wts['w_out'], wts['g_ffn'], wts['w_up'], wts['w_down']]
    const_specs = [_resident_spec(c.shape) for c in consts]
    return pl.pallas_call(
        functools.partial(_out_ffn_kernel, fc=fc),
        grid=(n // tm,),
        in_specs=[row(D_MODEL), row(GLA_V_W), row(DIFF_V_W)] + const_specs,
        out_specs=row(D_MODEL),
        out_shape=jax.ShapeDtypeStruct((n, D_MODEL), F32),
        compiler_params=_cparams(("parallel",)),
        name="out_ffn",
    )(x, mix_g, mix_d, *consts)


def _prep_weights(norm_mix_g, w_in, w_gk2, b_gk, gla_norm_g, q_norm_g, k_norm_g,
                  lambda_q1, lambda_k1, lambda_q2, lambda_k2, diff_norm_g, w_out,
                  norm_ffn_g, w_up, w_down):
    w = w_in[0]
    blk = np.arange(2 * LANES) // DH_DIFF
    ind = (np.arange(LANES)[:, None] // DK_GLA) == (np.arange(2 * DV_GLA)[None, :] // DV_GLA)
    return {
        'g_mix': norm_mix_g[0][None, :],
        'w_gla': w[:, :OFF_GLR].astype(BF16),
        'w_glr': jnp.pad(w[:, OFF_GLR:OFF_DIFF], ((0, 0), (0, LANES - GK_RANK))).astype(BF16),
        'w_gk2': jnp.pad(w_gk2[0], ((0, LANES - GK_RANK), (0, 0))).astype(BF16),
        'b_gk': b_gk[0][None, :],
        'w_diff': w[:, OFF_DIFF:].astype(BF16),
        'qn_g': jnp.tile(q_norm_g[0], 2 * H_DIFF)[None, :],
        'kn_g': jnp.tile(k_norm_g[0], 2 * H_DIFF)[None, :],
        'ones_blk': jnp.asarray(blk[:, None] == blk[None, :], BF16),
        'tri': jnp.asarray(np.tril(np.ones((GLA_C, GLA_C))), BF16),
        'ind': jnp.asarray(ind, BF16),
        'gla_g': gla_norm_g[0][None, :],
        'lq1': lambda_q1[0][None, :], 'lk1': lambda_k1[0][None, :],
        'lq2': lambda_q2[0][None, :], 'lk2': lambda_k2[0][None, :],
        'dn_g': diff_norm_g[0][None, :],
        'w_out': w_out[0].astype(BF16),
        'g_ffn': norm_ffn_g[0][None, :],
        'w_up': w_up[0].astype(BF16),
        'w_down': w_down[0].astype(BF16),
    }


def kernel(x_prompt, x_sample, cache_k, cache_v, page_table, state_gla, norm_mix_g, w_in, w_gk2, b_gk, gla_norm_g, q_norm_g, k_norm_g, lambda_q1, lambda_k1, lambda_q2, lambda_k2, diff_norm_g, w_out, norm_ffn_g, w_up, w_down):
    assert w_in.shape[0] == 1, "single-layer trunk"
    B, T, D = x_prompt.shape
    Bd, Td, _ = x_sample.shape
    assert Td == 1
    wts = _prep_weights(norm_mix_g, w_in, w_gk2, b_gk, gla_norm_g, q_norm_g, k_norm_g,
                        lambda_q1, lambda_k1, lambda_q2, lambda_k2, diff_norm_g, w_out,
                        norm_ffn_g, w_up, w_down)

    xp = x_prompt.reshape(B * T, D)
    xs = x_sample.reshape(Bd, D)
    qg, kg, vg, gg, la, qd, kdf_t, kdb_t, vdf_r, vdb = _in_proj(xp, wts, tm=TOK_TILE, seq=T)
    qg_s, kg_s, vg_s, gg_s, la_s, qd_s, kdf_s, _, vdf_s, _ = _in_proj(xs, wts, tm=Bd)

    mix_g, s_prompt = _gla_prompt(qg, kg, la, vg, gg, wts, B, T)
    mixg_s, s_sample = _gla_step(qg_s, kg_s, la_s, vg_s, gg_s, state_gla[0], wts)

    n_pool, page = cache_k.shape[1], cache_k.shape[2]
    cache_kt = jnp.transpose(cache_k[0], (0, 2, 3, 4, 1)).reshape(n_pool, DIFF_QK_W, page)
    cache_vr = cache_v[0].reshape(n_pool, page * H_DIFF, DV_DIFF)
    mix_d, mixd_s = _attention(qd, kdb_t, vdb, qd_s, kdf_s, vdf_s, cache_kt, cache_vr,
                               page_table, wts, B, T)

    y_prompt = _out_ffn(xp, mix_g, mix_d, wts, tm=TOK_TILE)
    y_sample = _out_ffn(xs, mixg_s, mixd_s, wts, tm=Bd)
    k_prompt = jnp.transpose(kdf_t.reshape(1, B, H_DIFF, 2, DH_DIFF, T), (0, 1, 5, 2, 3, 4))
    v_prompt = vdf_r.reshape(1, B, T, H_DIFF, DV_DIFF)

    return (y_prompt.reshape(B, T, D),
            y_sample.reshape(Bd, Td, D),
            k_prompt,
            v_prompt,
            s_prompt[None],
            kdf_s.reshape(1, Bd, Td, H_DIFF, 2, DH_DIFF),
            vdf_s.reshape(1, Bd, Td, H_DIFF, DV_DIFF),
            s_sample[None])
```

```python
import functools
import math

import jax
import jax.numpy as jnp
import numpy as np
from jax import lax
from jax.experimental import pallas as pl
from jax.experimental.pallas import tpu as pltpu

F32 = jnp.float32
BF16 = jnp.bfloat16

LANES = 128
SUBLANES = 8
VMEM_LIMIT = 56 * 1024 * 1024

D_MODEL = 1024
H_GLA = 4
DK_GLA = 64
DV_GLA = 128
GK_RANK = 16
GK_NORMALIZER = 16.0
H_DIFF = 4
DH_DIFF = 64
DV_DIFF = 128
D_FF = 4 * D_MODEL
EPS = 1e-6
LAMBDA_INIT = 0.8 - 0.6 * math.exp(-0.3 * 0)

GLA_QK_W = H_GLA * DK_GLA
GLA_V_W = H_GLA * DV_GLA
DIFF_QK_W = H_DIFF * 2 * DH_DIFF
DIFF_V_W = H_DIFF * DV_DIFF
OFF_GLR = 2 * GLA_QK_W + 2 * GLA_V_W
OFF_DIFF = OFF_GLR + GK_RANK

LOG2E = 1.4426950408889634
NEG_BIG = -1e30

GLA_C = 256
GLA_DIAG = 8
ATT_T = 512
ONES_ROWS = 16
TOK_TILE = 512


def _cparams(sem):
    return pltpu.CompilerParams(dimension_semantics=sem, vmem_limit_bytes=VMEM_LIMIT)


def _const_spec(shape):
    nd = len(shape)
    return pl.BlockSpec(shape, lambda *_: (0,) * nd)


def _resident_spec(shape):
    nd = len(shape)
    return pl.BlockSpec(shape, lambda *_: (0,) * nd, pipeline_mode=pl.Buffered(1))


def _dot(a, b):
    return jnp.dot(a, b, preferred_element_type=F32)


def _dot_nt(a, b):
    return lax.dot_general(a, b, (((1,), (1,)), ((), ())), preferred_element_type=F32)


def _lambda_full(lq1, lk1, lq2, lk2):
    a = jnp.sum(lq1 * lk1, axis=-1, keepdims=True)
    b = jnp.sum(lq2 * lk2, axis=-1, keepdims=True)
    return jnp.exp(a) - jnp.exp(b) + LAMBDA_INIT


def _group_rms(z, ones_blk, group):
    outs = []
    w = ones_blk.shape[0]
    for c in range(z.shape[1] // w):
        zc = z[:, c * w:(c + 1) * w]
        outs.append(_dot((zc * zc).astype(BF16), ones_blk))
    ss = jnp.concatenate(outs, axis=1)
    return lax.rsqrt(ss * (1.0 / group) + EPS)


def _in_proj_kernel(x_ref, g_ref, wgla_ref, wglr_ref, wgk2_ref, bgk_ref, wdiff_ref,
                    qng_ref, kng_ref, ones_ref,
                    qg_ref, kg_ref, vg_ref, gg_ref, la_ref,
                    qd_ref, kdf_ref, kdb_ref, vdf_ref, vdb_ref, *, cache_layout):
    x = x_ref[...]
    r = lax.rsqrt(jnp.mean(x * x, axis=-1, keepdims=True) + EPS)
    h = ((x * r) * g_ref[...]).astype(BF16)

    qg_ref[...] = _dot(h, wgla_ref[:, 0:GLA_QK_W]) * (DK_GLA ** -0.5)
    kg_ref[...] = _dot(h, wgla_ref[:, GLA_QK_W:2 * GLA_QK_W])
    vg_ref[...] = _dot(h, wgla_ref[:, 2 * GLA_QK_W:2 * GLA_QK_W + GLA_V_W])
    gg_ref[...] = _dot(h, wgla_ref[:, 2 * GLA_QK_W + GLA_V_W:])

    glr = _dot(h, wglr_ref[...])
    gl = _dot(glr.astype(BF16), wgk2_ref[...]) + bgk_ref[...]
    log_sig = jnp.minimum(gl, 0.0) - jnp.log1p(jnp.exp(-jnp.abs(gl)))
    la_ref[...] = log_sig / GK_NORMALIZER

    ones_blk = ones_ref[...]
    dq = _dot(h, wdiff_ref[:, 0:DIFF_QK_W])
    qn = (dq * _group_rms(dq, ones_blk, DH_DIFF)) * qng_ref[...]
    qs = qn * (DH_DIFF ** -0.5 * LOG2E)
    dk = _dot(h, wdiff_ref[:, DIFF_QK_W:2 * DIFF_QK_W])
    kn = (dk * _group_rms(dk, ones_blk, DH_DIFF)) * kng_ref[...]
    dv = _dot(h, wdiff_ref[:, 2 * DIFF_QK_W:])
    kdb_ref[...] = kn.astype(BF16)
    if cache_layout:
        qd_ref[0] = qs.T.astype(BF16)
        kdf_ref[0] = kn.T
        vdb_ref[0] = dv.T.astype(BF16)
        tm = dv.shape[0]
        for hh in range(H_DIFF):
            vdf_ref[pl.ds(hh, tm, stride=H_DIFF), :] = dv[:, hh * DV_DIFF:(hh + 1) * DV_DIFF]
    else:
        qd_ref[...] = qs.astype(BF16)
        kdf_ref[...] = kn
        vdf_ref[...] = dv
        vdb_ref[...] = dv.astype(BF16)


def _in_proj(x, wts, tm, seq=None):
    n = x.shape[0]
    assert n % tm == 0
    row = lambda w: pl.BlockSpec((tm, w), lambda i: (i, 0))
    out_w = [(GLA_QK_W, F32), (GLA_QK_W, F32), (GLA_V_W, F32), (GLA_V_W, F32), (GLA_QK_W, F32),
             (DIFF_QK_W, BF16), (DIFF_QK_W, F32), (DIFF_QK_W, BF16), (DIFF_V_W, F32), (DIFF_V_W, BF16)]
    out_specs = [row(w) for w, _ in out_w]
    out_shape = [jax.ShapeDtypeStruct((n, w), dt) for w, dt in out_w]
    if seq is not None:
        assert seq % tm == 0 and n % seq == 0
        per = seq // tm
        kt_spec = pl.BlockSpec((1, DIFF_QK_W, tm), lambda i: (i // per, 0, i % per))
        out_specs[5] = out_specs[6] = out_specs[9] = kt_spec
        out_shape[5] = jax.ShapeDtypeStruct((n // seq, DIFF_QK_W, seq), BF16)
        out_shape[6] = jax.ShapeDtypeStruct((n // seq, DIFF_QK_W, seq), F32)
        out_shape[9] = jax.ShapeDtypeStruct((n // seq, DIFF_V_W, seq), BF16)
        out_specs[8] = pl.BlockSpec((tm * H_DIFF, DV_DIFF), lambda i: (i, 0))
        out_shape[8] = jax.ShapeDtypeStruct((n * H_DIFF, DV_DIFF), F32)
    consts = [wts['g_mix'], wts['w_gla'], wts['w_glr'], wts['w_gk2'], wts['b_gk'], wts['w_diff'],
              wts['qn_g'], wts['kn_g'], wts['ones_blk']]
    return pl.pallas_call(
        functools.partial(_in_proj_kernel, cache_layout=seq is not None),
        grid=(n // tm,),
        in_specs=[row(D_MODEL)] + [_resident_spec(c.shape) for c in consts],
        out_specs=out_specs,
        out_shape=out_shape,
        compiler_params=_cparams(("parallel",)),
        name="in_proj",
    )(x, *consts)


def _bcast_row(x3, j):
    return jnp.broadcast_to(x3[:, j:j + 1, :], x3.shape)


def _gla_prompt_kernel(q_ref, k_ref, la_ref, v_ref, gate_ref, tri_ref, ind_ref, gn_ref,
                       o_ref, s_ref, st_ref):
    c = pl.program_id(2)
    C = q_ref.shape[0]
    neg_inf = -jnp.inf

    @pl.when(c == 0)
    def _():
        st_ref[...] = jnp.zeros_like(st_ref)

    la = la_ref[...]
    tri = tri_ref[...]
    t1 = la.astype(BF16)
    r1 = la - t1.astype(F32)
    t2 = r1.astype(BF16)
    t3 = (r1 - t2.astype(F32)).astype(BF16)
    b = _dot(tri, t1) + _dot(tri, t2) + _dot(tri, t3)

    q = q_ref[...]
    k = k_ref[...]
    v = v_ref[...]
    vb = v.astype(BF16)
    lane = lax.broadcasted_iota(jnp.int32, (C, LANES), 1)
    head0 = lane < DK_GLA

    G = C // GLA_DIAG
    b3 = b.reshape(G, GLA_DIAG, LANES)
    q3 = q.reshape(G, GLA_DIAG, LANES)
    k3 = k.reshape(G, GLA_DIAG, LANES)
    v3 = v.reshape(G, GLA_DIAG, 2 * DV_GLA)
    sub = lax.broadcasted_iota(jnp.int32, (G, GLA_DIAG, LANES), 1)
    ind = ind_ref[...]
    o = jnp.zeros((C, 2 * DV_GLA), F32)
    for j in range(GLA_DIAG):
        e = jnp.exp(jnp.where(sub >= j, b3 - _bcast_row(b3, j), neg_inf))
        pj = (q3 * _bcast_row(k3, j) * e).reshape(C, LANES).astype(BF16)
        rj = _dot(pj, ind)
        o = o + rj * _bcast_row(v3, j).reshape(C, 2 * DV_GLA)

    ts_xor = (lax.broadcasted_iota(jnp.int32, (C, C), 0)
              ^ lax.broadcasted_iota(jnp.int32, (C, C), 1))
    tok = lax.broadcasted_iota(jnp.int32, (C, LANES), 0)
    a0 = jnp.zeros((C, C), F32)
    a1 = jnp.zeros((C, C), F32)
    m = GLA_DIAG
    while m < C:
        P = C // (2 * m)
        bmid = _bcast_row(b.reshape(P, 2 * m, LANES), m).reshape(C, LANES)
        late = (tok & m) != 0
        qt = q * jnp.exp(jnp.where(late, b - bmid, neg_inf))
        kt = (k * jnp.exp(jnp.where(late, neg_inf, bmid - b))).astype(BF16)
        split_here = (ts_xor // m) == 1
        x0 = _dot_nt(jnp.where(head0, qt, 0.0).astype(BF16), kt)
        x1 = _dot_nt(jnp.where(head0, 0.0, qt).astype(BF16), kt)
        a0 = jnp.where(split_here, x0, a0)
        a1 = jnp.where(split_here, x1, a1)
        m *= 2
    o_intra = jnp.concatenate([_dot(a0.astype(BF16), vb[:, :DV_GLA]),
                               _dot(a1.astype(BF16), vb[:, DV_GLA:])], axis=1)

    st = st_ref[...]
    stb = st.astype(BF16)
    qe = q * jnp.exp(b)
    o_inter = jnp.concatenate(
        [_dot_nt(jnp.where(head0, qe, 0.0).astype(BF16), stb[:DV_GLA]),
         _dot_nt(jnp.where(head0, 0.0, qe).astype(BF16), stb[DV_GLA:])], axis=1)
    o = o + o_intra + o_inter

    b_last = b[C - 1:C, :]
    kd = (k * jnp.exp(b_last - b)).astype(BF16)
    kv0 = _dot(vb[:, :DV_GLA].T, kd)
    kv1 = _dot(vb[:, DV_GLA:].T, kd)
    head0_s = lax.broadcasted_iota(jnp.int32, (DV_GLA, LANES), 1) < DK_GLA
    dec = jnp.exp(b_last)
    st_new = jnp.concatenate([st[:DV_GLA] * dec + jnp.where(head0_s, kv0, 0.0),
                              st[DV_GLA:] * dec + jnp.where(head0_s, 0.0, kv1)], axis=0)
    st_ref[...] = st_new

    gate = gate_ref[...]
    gn = gn_ref[...]
    outs = []
    for hh in range(2):
        oh = o[:, hh * DV_GLA:(hh + 1) * DV_GLA]
        gh = gate[:, hh * DV_GLA:(hh + 1) * DV_GLA]
        r = lax.rsqrt(jnp.mean(oh * oh, axis=-1, keepdims=True) + EPS)
        outs.append(((oh * r) * gn) * (gh * (1.0 / (1.0 + jnp.exp(-gh)))))
    o_ref[...] = jnp.concatenate(outs, axis=1).astype(o_ref.dtype)

    @pl.when(c == pl.num_programs(2) - 1)
    def _():
        s0 = st_new[:DV_GLA].T
        s1 = st_new[DV_GLA:].T
        s_ref[0, 0] = s0[:DK_GLA]
        s_ref[0, 1] = s1[DK_GLA:]


def _gla_prompt(qg, kg, la, vg, gg, wts, batch, seq):
    C = GLA_C
    assert seq % C == 0
    nc = seq // C
    npair = H_GLA // 2
    qk_spec = pl.BlockSpec((C, LANES), lambda b, p, c: (b * nc + c, p))
    v_spec = pl.BlockSpec((C, 2 * DV_GLA), lambda b, p, c: (b * nc + c, p))
    return pl.pallas_call(
        _gla_prompt_kernel,
        grid=(batch, npair, nc),
        in_specs=[qk_spec, qk_spec, qk_spec, v_spec, v_spec,
                  _const_spec((C, C)), _const_spec((LANES, 2 * DV_GLA)), _const_spec((1, DV_GLA))],
        out_specs=[v_spec, pl.BlockSpec((1, 2, DK_GLA, DV_GLA), lambda b, p, c: (b, p, 0, 0))],
        out_shape=[jax.ShapeDtypeStruct((batch * seq, GLA_V_W), BF16),
                   jax.ShapeDtypeStruct((batch, H_GLA, DK_GLA, DV_GLA), F32)],
        scratch_shapes=[pltpu.VMEM((2 * DV_GLA, LANES), F32)],
        compiler_params=_cparams(("parallel", "parallel", "arbitrary")),
        name="gla_prompt",
    )(qg, kg, la, vg, gg, wts['tri'], wts['ind'], wts['gla_g'])


def _gla_step_kernel(q_ref, k_ref, la_ref, v_ref, gate_ref, s_ref, gn_ref, o_ref, so_ref):
    b = pl.program_id(0)
    nb = q_ref.shape[0]

    def column(ref):
        x = ref[...]
        pad = jnp.zeros((LANES - nb, x.shape[1]), F32)
        xt = jnp.concatenate([x, pad], axis=0).T
        lane = lax.broadcasted_iota(jnp.int32, xt.shape, 1)
        return jnp.sum(jnp.where(lane == b, xt, 0.0), axis=1, keepdims=True)

    qc = column(q_ref)
    kc = column(k_ref)
    ac = jnp.exp(column(la_ref))
    v = v_ref[0]
    gate = gate_ref[0]
    gn = gn_ref[...]
    outs = []
    for h in range(H_GLA):
        sl = slice(h * DK_GLA, (h + 1) * DK_GLA)
        vs = slice(h * DV_GLA, (h + 1) * DV_GLA)
        s_new = s_ref[0, h] * ac[sl] + kc[sl] * v[:, vs]
        so_ref[0, h] = s_new
        oh = jnp.sum(qc[sl] * s_new, axis=0, keepdims=True)
        r = lax.rsqrt(jnp.mean(oh * oh, axis=-1, keepdims=True) + EPS)
        gh = gate[:, vs]
        outs.append(((oh * r) * gn) * (gh * (1.0 / (1.0 + jnp.exp(-gh)))))
    o_ref[0] = jnp.concatenate(outs, axis=1)


def _gla_step(qg, kg, la, vg, gg, state, wts):
    nb = qg.shape[0]
    assert nb <= LANES
    full = _const_spec((nb, GLA_QK_W))
    row3 = pl.BlockSpec((1, 1, GLA_V_W), lambda b: (b, 0, 0))
    st_spec = pl.BlockSpec((1, H_GLA, DK_GLA, DV_GLA), lambda b: (b, 0, 0, 0))
    o, s = pl.pallas_call(
        _gla_step_kernel,
        grid=(nb,),
        in_specs=[full, full, full, row3, row3, st_spec, _const_spec((1, DV_GLA))],
        out_specs=[row3, st_spec],
        out_shape=[jax.ShapeDtypeStruct((nb, 1, GLA_V_W), F32),
                   jax.ShapeDtypeStruct(state.shape, F32)],
        compiler_params=_cparams(("parallel",)),
        name="gla_step",
    )(qg, kg, la, vg.reshape(nb, 1, GLA_V_W), gg.reshape(nb, 1, GLA_V_W), state, wts['gla_g'])
    return o.reshape(nb, GLA_V_W), s


def _diff_finish(o1, o2, lam, dn_g):
    o = o1 - lam * o2
    r = lax.rsqrt(jnp.mean(o * o, axis=-1, keepdims=True) + EPS)
    return ((o * r) * dn_g) * (1.0 - LAMBDA_INIT)


def _decode_step(j, nj, lam, q_ref, kn_ref, vn_ref, dn_ref, k_refs, v_refs, o_ref,
                 qbd_ref, m_ref, l_ref, acc_ref):
    R = 2 * H_DIFF
    rowi = lax.broadcasted_iota(jnp.int32, (R, DIFF_QK_W), 0)
    lanei = lax.broadcasted_iota(jnp.int32, (R, DIFF_QK_W), 1)

    @pl.when(j == 0)
    def _():
        qb = jnp.broadcast_to(q_ref[0], (R, DIFF_QK_W))
        qbd_ref[...] = jnp.where(lanei // DH_DIFF == rowi, qb, 0.0).astype(BF16)
        m_ref[...] = jnp.full_like(m_ref, NEG_BIG)
        l_ref[...] = jnp.zeros_like(l_ref)
        acc_ref[...] = jnp.zeros_like(acc_ref)

    qbd = qbd_ref[...]
    s = jnp.concatenate([_dot(qbd, kr[0].astype(BF16)) for kr in k_refs], axis=1)
    m_old = m_ref[...]
    m_new = jnp.maximum(m_old, jnp.max(s, axis=-1, keepdims=True))
    alpha = jnp.exp2(m_old - m_new)
    p = jnp.exp2(s - m_new)
    l_new = alpha * l_ref[...] + jnp.sum(p, axis=-1, keepdims=True)
    pb = p.astype(BF16)
    acc_ref[...] = alpha * acc_ref[...]
    for i, vr in enumerate(v_refs):
        pi = pb[:, i * LANES:(i + 1) * LANES]
        for h in range(H_DIFF):
            vh = vr.at[0][pl.ds(h, LANES, stride=H_DIFF), :]
            acc_ref[:, h * DV_DIFF:(h + 1) * DV_DIFF] += _dot(pi, vh.astype(BF16))
    acc_new = acc_ref[...]
    m_ref[...] = m_new
    l_ref[...] = l_new

    @pl.when(j == nj - 1)
    def _():
        s_n = jnp.sum(qbd.astype(F32) * kn_ref[0], axis=-1, keepdims=True)
        m_f = jnp.maximum(m_new, s_n)
        a_f = jnp.exp2(m_new - m_f)
        p_n = jnp.exp2(s_n - m_f)
        l_f = a_f * l_new + p_n
        acc_f = a_f * acc_new + p_n * vn_ref[0]
        coef = jnp.where((rowi & 1) == 0, 1.0, -lam)
        w = jnp.where(lanei // DV_DIFF == rowi // 2, (acc_f / l_f) * coef, 0.0)
        o = jnp.sum(w, axis=0, keepdims=True)
        dn = dn_ref[...]
        outs = []
        for h in range(H_DIFF):
            oh = o[:, h * DV_DIFF:(h + 1) * DV_DIFF]
            outs.append(_diff_finish(oh, jnp.zeros_like(oh), lam, dn))
        o_ref[0] = jnp.concatenate(outs, axis=1)


def _attention_kernel(pt_ref, qt_ref, k_ref, vt_ref, lq1, lk1, lq2, lk2, dn_ref, dnc_ref,
                      dq_ref, dkn_ref, dvn_ref, *rest, n_pages_step, n_chunks):
    del pt_ref
    k_refs = rest[:n_pages_step]
    v_refs = rest[n_pages_step:2 * n_pages_step]
    (o_ref, do_ref, m_ref, acc_ref, sa_ref, sb_ref,
     qbd_ref, dm_ref, dl_ref, dacc_ref) = rest[2 * n_pages_step:]
    qi = pl.program_id(2)
    lam = _lambda_full(lq1[...], lk1[...], lq2[...], lk2[...])

    step = (pl.program_id(0) * pl.num_programs(1) + pl.program_id(1)) * pl.num_programs(2) + qi
    _decode_step(lax.rem(step, n_chunks), n_chunks, lam, dq_ref, dkn_ref, dvn_ref, dn_ref,
                 k_refs, v_refs, do_ref, qbd_ref, dm_ref, dl_ref, dacc_ref)

    T = qt_ref.shape[2]
    qt = qt_ref[0]
    dim = lax.broadcasted_iota(jnp.int32, (LANES, T), 0)
    zero = jnp.zeros_like(qt)
    qq = jnp.concatenate([jnp.where(dim < DH_DIFF, qt, zero),
                          jnp.where(dim < DH_DIFF, zero, qt)], axis=1)
    ones = jnp.ones((ONES_ROWS, T), BF16)
    key_i = lax.broadcasted_iota(jnp.int32, (T, 2 * T), 0)
    qry_i = lax.broadcasted_iota(jnp.int32, (T, 2 * T), 1) & (T - 1)
    causal = key_i <= qry_i

    def scores(tile, s_ref):
        s_ref[...] = _dot(k_ref[pl.ds(pl.multiple_of(tile * T, T), T), :], qq)

    def update(tile, s_ref, masked):
        v1 = jnp.concatenate([vt_ref[0, :, pl.ds(pl.multiple_of(tile * T, T), T)], ones], axis=0)
        s = s_ref[...]
        if masked:
            s = jnp.where(causal, s, -jnp.inf)
        m_old = m_ref[...]
        m_new = jnp.maximum(m_old, jnp.max(s, axis=0, keepdims=True))
        alpha = jnp.exp2(m_old - m_new)
        p = jnp.exp2(s - m_new)
        acc_ref[...] = alpha * acc_ref[...] + _dot(v1, p.astype(BF16))
        m_ref[...] = m_new

    m_ref[...] = jnp.full_like(m_ref, -jnp.inf)
    acc_ref[...] = jnp.zeros_like(acc_ref)
    scores(0, sa_ref)

    def body(j, carry):
        t = 2 * j
        scores(t + 1, sb_ref)
        update(t, sa_ref, False)
        scores(t + 2, sa_ref)
        update(t + 1, sb_ref, False)
        return carry

    lax.fori_loop(0, lax.shift_right_logical(qi, 1), body, 0)

    @pl.when((qi & 1) == 0)
    def _():
        update(qi, sa_ref, True)

    @pl.when((qi & 1) == 1)
    def _():
        scores(qi, sb_ref)
        update(qi - 1, sa_ref, False)
        update(qi, sb_ref, True)

    acc = acc_ref[...]
    on = acc[:DV_DIFF] / acc[DV_DIFF:DV_DIFF + 1]
    o = on[:, :T] - lam * on[:, T:]
    r = lax.rsqrt(jnp.mean(o * o, axis=0, keepdims=True) + EPS)
    o = ((o * r) * dnc_ref[...]) * (1.0 - LAMBDA_INIT)
    o_ref[...] = o.T.astype(o_ref.dtype)


def _attention(qdt, kd, vdt, qd_s, kd_s, vd_s, cache_kt, cache_vr, page_table, wts, batch, seq):
    T = ATT_T
    assert seq % T == 0 and (T & (T - 1)) == 0
    nq = seq // T
    nb, n_pages = page_table.shape
    assert cache_kt.shape[2] == LANES
    n_steps = batch * H_DIFF * nq
    assert (nb * n_pages) % n_steps == 0
    pps = (nb * n_pages) // n_steps
    assert n_pages % pps == 0
    n_chunks = n_pages // pps

    def step_of(b, h, i):
        return (b * H_DIFF + h) * nq + i

    o_spec = pl.BlockSpec((T, LANES), lambda b, h, i, pt: (b * nq + i, h))
    qt_spec = pl.BlockSpec((1, LANES, T), lambda b, h, i, pt: (b, h, i))
    k_spec = pl.BlockSpec((seq, LANES), lambda b, h, i, pt: (b, h))
    vt_spec = pl.BlockSpec((1, LANES, seq), lambda b, h, i, pt: (b, h, 0))
    lam_spec = pl.BlockSpec((1, DH_DIFF), lambda b, h, i, pt: (0, 0))
    row3 = pl.BlockSpec((1, 1, DIFF_QK_W), lambda b, h, i, pt: (step_of(b, h, i) // n_chunks, 0, 0))

    def page_spec(k):
        return pl.BlockSpec((1, DIFF_QK_W, LANES),
                            lambda b, h, i, pt: (pt[step_of(b, h, i) * pps + k], 0, 0))

    grid_spec = pltpu.PrefetchScalarGridSpec(
        num_scalar_prefetch=1,
        grid=(batch, H_DIFF, nq),
        in_specs=[qt_spec, k_spec, vt_spec, lam_spec, lam_spec, lam_spec, lam_spec,
                  pl.BlockSpec((1, DV_DIFF), lambda b, h, i, pt: (0, 0)),
                  pl.BlockSpec((DV_DIFF, 1), lambda b, h, i, pt: (0, 0)), row3, row3, row3]
                 + [page_spec(k) for k in range(pps)] * 2,
        out_specs=[o_spec, row3],
        scratch_shapes=[pltpu.VMEM((1, 2 * T), F32), pltpu.VMEM((DV_DIFF + ONES_ROWS, 2 * T), F32),
                        pltpu.VMEM((T, 2 * T), F32), pltpu.VMEM((T, 2 * T), F32),
                        pltpu.VMEM((2 * H_DIFF, DIFF_QK_W), BF16),
                        pltpu.VMEM((2 * H_DIFF, 1), F32), pltpu.VMEM((2 * H_DIFF, 1), F32),
                        pltpu.VMEM((2 * H_DIFF, DIFF_V_W), F32)],
    )
    o, o_s = pl.pallas_call(
        functools.partial(_attention_kernel, n_pages_step=pps, n_chunks=n_chunks),
        grid_spec=grid_spec,
        out_shape=[jax.ShapeDtypeStruct((batch * seq, DIFF_V_W), BF16),
                   jax.ShapeDtypeStruct((nb, 1, DIFF_V_W), F32)],
        compiler_params=_cparams(("arbitrary", "arbitrary", "arbitrary")),
        name="attention",
    )(page_table.reshape(-1), qdt, kd, vdt,
      wts['lq1'], wts['lk1'], wts['lq2'], wts['lk2'], wts['dn_g'], wts['dn_g'].reshape(DV_DIFF, 1),
      qd_s.astype(F32).reshape(nb, 1, DIFF_QK_W), kd_s.reshape(nb, 1, DIFF_QK_W),
      vd_s.reshape(nb, 1, DIFF_V_W), *([cache_kt] * pps), *([cache_vr] * pps))
    return o, o_s.reshape(nb, DIFF_V_W)


def _out_ffn_kernel(x_ref, mg_ref, md_ref, wo_ref, g_ref, wup_ref, wdn_ref, y_ref, *, fc):
    half = GLA_V_W
    x1 = (x_ref[...] + _dot(mg_ref[...].astype(BF16), wo_ref[0:half, :])
          + _dot(md_ref[...].astype(BF16), wo_ref[half:, :]))
    r = lax.rsqrt(jnp.mean(x1 * x1, axis=-1, keepdims=True) + EPS)
    h2 = ((x1 * r) * g_ref[...]).astype(BF16)
    y_ref[...] = x1
    for f in range(D_FF // fc):
        u = jnp.maximum(_dot(h2, wup_ref[:, f * fc:(f + 1) * fc]), 0.0)
        y_ref[...] += _dot((u * u).astype(BF16), wdn_ref[f * fc:(f + 1) * fc, :])


def _out_ffn(x, mix_g, mix_d, wts, tm, fc=512):
    n = x.shape[0]
    assert n % tm == 0
    row = lambda w: pl.BlockSpec((tm, w), lambda i: (i, 0))
    consts = [wts['w_out'], wts['g_ffn'], wts['w_up'], wts['w_down']]
    const_specs = [_resident_spec(c.shape) for c in consts]
    return pl.pallas_call(
        functools.partial(_out_ffn_kernel, fc=fc),
        grid=(n // tm,),
        in_specs=[row(D_MODEL), row(GLA_V_W), row(DIFF_V_W)] + const_specs,
        out_specs=row(D_MODEL),
        out_shape=jax.ShapeDtypeStruct((n, D_MODEL), F32),
        compiler_params=_cparams(("parallel",)),
        name="out_ffn",
    )(x, mix_g, mix_d, *consts)


def _prep_weights(norm_mix_g, w_in, w_gk2, b_gk, gla_norm_g, q_norm_g, k_norm_g,
                  lambda_q1, lambda_k1, lambda_q2, lambda_k2, diff_norm_g, w_out,
                  norm_ffn_g, w_up, w_down):
    w = w_in[0]
    blk = np.arange(2 * LANES) // DH_DIFF
    ind = (np.arange(LANES)[:, None] // DK_GLA) == (np.arange(2 * DV_GLA)[None, :] // DV_GLA)
    return {
        'g_mix': norm_mix_g[0][None, :],
        'w_gla': w[:, :OFF_GLR].astype(BF16),
        'w_glr': jnp.pad(w[:, OFF_GLR:OFF_DIFF], ((0, 0), (0, LANES - GK_RANK))).astype(BF16),
        'w_gk2': jnp.pad(w_gk2[0], ((0, LANES - GK_RANK), (0, 0))).astype(BF16),
        'b_gk': b_gk[0][None, :],
        'w_diff': w[:, OFF_DIFF:].astype(BF16),
        'qn_g': jnp.tile(q_norm_g[0], 2 * H_DIFF)[None, :],
        'kn_g': jnp.tile(k_norm_g[0], 2 * H_DIFF)[None, :],
        'ones_blk': jnp.asarray(blk[:, None] == blk[None, :], BF16),
        'tri': jnp.asarray(np.tril(np.ones((GLA_C, GLA_C))), BF16),
        'ind': jnp.asarray(ind, BF16),
        'gla_g': gla_norm_g[0][None, :],
        'lq1': lambda_q1[0][None, :], 'lk1': lambda_k1[0][None, :],
        'lq2': lambda_q2[0][None, :], 'lk2': lambda_k2[0][None, :],
        'dn_g': diff_norm_g[0][None, :],
        'w_out': w_out[0].astype(BF16),
        'g_ffn': norm_ffn_g[0][None, :],
        'w_up': w_up[0].astype(BF16),
        'w_down': w_down[0].astype(BF16),
    }


def kernel(x_prompt, x_sample, cache_k, cache_v, page_table, state_gla, norm_mix_g, w_in, w_gk2, b_gk, gla_norm_g, q_norm_g, k_norm_g, lambda_q1, lambda_k1, lambda_q2, lambda_k2, diff_norm_g, w_out, norm_ffn_g, w_up, w_down):
    assert w_in.shape[0] == 1, "single-layer trunk"
    B, T, D = x_prompt.shape
    Bd, Td, _ = x_sample.shape
    assert Td == 1
    wts = _prep_weights(norm_mix_g, w_in, w_gk2, b_gk, gla_norm_g, q_norm_g, k_norm_g,
                        lambda_q1, lambda_k1, lambda_q2, lambda_k2, diff_norm_g, w_out,
                        norm_ffn_g, w_up, w_down)

    xp = x_prompt.reshape(B * T, D)
    xs = x_sample.reshape(Bd, D)
    qg, kg, vg, gg, la, qd_t, kdf_t, kdb, vdf_r, vdb_t = _in_proj(xp, wts, tm=TOK_TILE, seq=T)
    qg_s, kg_s, vg_s, gg_s, la_s, qd_s, kdf_s, _, vdf_s, _ = _in_proj(xs, wts, tm=Bd)

    mix_g, s_prompt = _gla_prompt(qg, kg, la, vg, gg, wts, B, T)
    mixg_s, s_sample = _gla_step(qg_s, kg_s, la_s, vg_s, gg_s, state_gla[0], wts)

    n_pool, page = cache_k.shape[1], cache_k.shape[2]
    cache_kt = jnp.transpose(cache_k[0], (0, 2, 3, 4, 1)).reshape(n_pool, DIFF_QK_W, page)
    cache_vr = cache_v[0].reshape(n_pool, page * H_DIFF, DV_DIFF)
    mix_d, mixd_s = _attention(qd_t, kdb, vdb_t, qd_s, kdf_s, vdf_s, cache_kt, cache_vr,
                               page_table, wts, B, T)

    y_prompt = _out_ffn(xp, mix_g, mix_d, wts, tm=TOK_TILE)
    y_sample = _out_ffn(xs, mixg_s, mixd_s, wts, tm=Bd)
    k_prompt = jnp.transpose(kdf_t.reshape(1, B, H_DIFF, 2, DH_DIFF, T), (0, 1, 5, 2, 3, 4))
    v_prompt = vdf_r.reshape(1, B, T, H_DIFF, DV_DIFF)

    return (y_prompt.reshape(B, T, D),
            y_sample.reshape(Bd, Td, D),
            k_prompt,
            v_prompt,
            s_prompt[None],
            kdf_s.reshape(1, Bd, Td, H_DIFF, 2, DH_DIFF),
            vdf_s.reshape(1, Bd, Td, H_DIFF, DV_DIFF),
            s_sample[None])
```

```python
import functools
import math

import jax
import jax.numpy as jnp
import numpy as np
from jax import lax
from jax.experimental import pallas as pl
from jax.experimental.pallas import tpu as pltpu

F32 = jnp.float32
BF16 = jnp.bfloat16

LANES = 128
SUBLANES = 8
VMEM_LIMIT = 56 * 1024 * 1024

D_MODEL = 1024
H_GLA = 4
DK_GLA = 64
DV_GLA = 128
GK_RANK = 16
GK_NORMALIZER = 16.0
H_DIFF = 4
DH_DIFF = 64
DV_DIFF = 128
D_FF = 4 * D_MODEL
EPS = 1e-6
LAMBDA_INIT = 0.8 - 0.6 * math.exp(-0.3 * 0)

GLA_QK_W = H_GLA * DK_GLA
GLA_V_W = H_GLA * DV_GLA
DIFF_QK_W = H_DIFF * 2 * DH_DIFF
DIFF_V_W = H_DIFF * DV_DIFF
OFF_GLR = 2 * GLA_QK_W + 2 * GLA_V_W
OFF_DIFF = OFF_GLR + GK_RANK

LOG2E = 1.4426950408889634
NEG_BIG = -1e30

GLA_C = 256
GLA_DIAG = 8
ATT_T = 512
ONES_ROWS = 16
TOK_TILE = 512


def _cparams(sem):
    return pltpu.CompilerParams(dimension_semantics=sem, vmem_limit_bytes=VMEM_LIMIT)


def _const_spec(shape):
    nd = len(shape)
    return pl.BlockSpec(shape, lambda *_: (0,) * nd)


def _resident_spec(shape):
    nd = len(shape)
    return pl.BlockSpec(shape, lambda *_: (0,) * nd, pipeline_mode=pl.Buffered(1))


def _dot(a, b):
    return jnp.dot(a, b, preferred_element_type=F32)


def _dot_nt(a, b):
    return lax.dot_general(a, b, (((1,), (1,)), ((), ())), preferred_element_type=F32)


def _lambda_full(lq1, lk1, lq2, lk2):
    a = jnp.sum(lq1 * lk1, axis=-1, keepdims=True)
    b = jnp.sum(lq2 * lk2, axis=-1, keepdims=True)
    return jnp.exp(a) - jnp.exp(b) + LAMBDA_INIT


def _group_rms(z, ones_blk, group):
    outs = []
    w = ones_blk.shape[0]
    for c in range(z.shape[1] // w):
        zc = z[:, c * w:(c + 1) * w]
        outs.append(_dot((zc * zc).astype(BF16), ones_blk))
    ss = jnp.concatenate(outs, axis=1)
    return lax.rsqrt(ss * (1.0 / group) + EPS)


def _in_proj_kernel(x_ref, g_ref, wgla_ref, wglr_ref, wgk2_ref, bgk_ref, wdiff_ref,
                    qng_ref, kng_ref, ones_ref,
                    qg_ref, kg_ref, vg_ref, gg_ref, la_ref,
                    qd_ref, kdf_ref, kdb_ref, vdf_ref, vdb_ref, *, cache_layout):
    x = x_ref[...]
    r = lax.rsqrt(jnp.mean(x * x, axis=-1, keepdims=True) + EPS)
    h = ((x * r) * g_ref[...]).astype(BF16)

    qg_ref[...] = _dot(h, wgla_ref[:, 0:GLA_QK_W]) * (DK_GLA ** -0.5)
    kg_ref[...] = _dot(h, wgla_ref[:, GLA_QK_W:2 * GLA_QK_W])
    vg_ref[...] = _dot(h, wgla_ref[:, 2 * GLA_QK_W:2 * GLA_QK_W + GLA_V_W])
    gg_ref[...] = _dot(h, wgla_ref[:, 2 * GLA_QK_W + GLA_V_W:])

    glr = _dot(h, wglr_ref[...])
    gl = _dot(glr.astype(BF16), wgk2_ref[...]) + bgk_ref[...]
    log_sig = jnp.minimum(gl, 0.0) - jnp.log1p(jnp.exp(-jnp.abs(gl)))
    la_ref[...] = log_sig / GK_NORMALIZER

    ones_blk = ones_ref[...]
    dq = _dot(h, wdiff_ref[:, 0:DIFF_QK_W])
    qn = (dq * _group_rms(dq, ones_blk, DH_DIFF)) * qng_ref[...]
    qs = qn * (DH_DIFF ** -0.5 * LOG2E)
    dk = _dot(h, wdiff_ref[:, DIFF_QK_W:2 * DIFF_QK_W])
    kn = (dk * _group_rms(dk, ones_blk, DH_DIFF)) * kng_ref[...]
    dv = _dot(h, wdiff_ref[:, 2 * DIFF_QK_W:])
    kdb_ref[...] = kn.astype(BF16)
    if cache_layout:
        qd_ref[0] = qs.T.astype(BF16)
        kdf_ref[0] = kn.T
        vdb_ref[0] = dv.T.astype(BF16)
        tm = dv.shape[0]
        for hh in range(H_DIFF):
            vdf_ref[pl.ds(hh, tm, stride=H_DIFF), :] = dv[:, hh * DV_DIFF:(hh + 1) * DV_DIFF]
    else:
        qd_ref[...] = qs.astype(BF16)
        kdf_ref[...] = kn
        vdf_ref[...] = dv
        vdb_ref[...] = dv.astype(BF16)


def _in_proj(x, wts, tm, seq=None):
    n = x.shape[0]
    assert n % tm == 0
    row = lambda w: pl.BlockSpec((tm, w), lambda i: (i, 0))
    out_w = [(GLA_QK_W, F32), (GLA_QK_W, F32), (GLA_V_W, F32), (GLA_V_W, F32), (GLA_QK_W, F32),
             (DIFF_QK_W, BF16), (DIFF_QK_W, F32), (DIFF_QK_W, BF16), (DIFF_V_W, F32), (DIFF_V_W, BF16)]
    out_specs = [row(w) for w, _ in out_w]
    out_shape = [jax.ShapeDtypeStruct((n, w), dt) for w, dt in out_w]
    if seq is not None:
        assert seq % tm == 0 and n % seq == 0
        per = seq // tm
        kt_spec = pl.BlockSpec((1, DIFF_QK_W, tm), lambda i: (i // per, 0, i % per))
        out_specs[5] = out_specs[6] = out_specs[9] = kt_spec
        out_shape[5] = jax.ShapeDtypeStruct((n // seq, DIFF_QK_W, seq), BF16)
        out_shape[6] = jax.ShapeDtypeStruct((n // seq, DIFF_QK_W, seq), F32)
        out_shape[9] = jax.ShapeDtypeStruct((n // seq, DIFF_V_W, seq), BF16)
        out_specs[8] = pl.BlockSpec((tm * H_DIFF, DV_DIFF), lambda i: (i, 0))
        out_shape[8] = jax.ShapeDtypeStruct((n * H_DIFF, DV_DIFF), F32)
    consts = [wts['g_mix'], wts['w_gla'], wts['w_glr'], wts['w_gk2'], wts['b_gk'], wts['w_diff'],
              wts['qn_g'], wts['kn_g'], wts['ones_blk']]
    return pl.pallas_call(
        functools.partial(_in_proj_kernel, cache_layout=seq is not None),
        grid=(n // tm,),
        in_specs=[row(D_MODEL)] + [_resident_spec(c.shape) for c in consts],
        out_specs=out_specs,
        out_shape=out_shape,
        compiler_params=_cparams(("parallel",)),
        name="in_proj",
    )(x, *consts)


def _bcast_row(x3, j):
    return jnp.broadcast_to(x3[:, j:j + 1, :], x3.shape)


def _gla_prompt_kernel(q_ref, k_ref, la_ref, v_ref, gate_ref, tri_ref, ind_ref, gn_ref,
                       o_ref, s_ref, st_ref):
    c = pl.program_id(2)
    C = q_ref.shape[0]
    neg_inf = -jnp.inf

    @pl.when(c == 0)
    def _():
        st_ref[...] = jnp.zeros_like(st_ref)

    la = la_ref[...]
    tri = tri_ref[...]
    t1 = la.astype(BF16)
    r1 = la - t1.astype(F32)
    t2 = r1.astype(BF16)
    t3 = (r1 - t2.astype(F32)).astype(BF16)
    b = _dot(tri, t1) + _dot(tri, t2) + _dot(tri, t3)

    q = q_ref[...]
    k = k_ref[...]
    v = v_ref[...]
    vb = v.astype(BF16)
    lane = lax.broadcasted_iota(jnp.int32, (C, LANES), 1)
    head0 = lane < DK_GLA

    G = C // GLA_DIAG
    b3 = b.reshape(G, GLA_DIAG, LANES)
    q3 = q.reshape(G, GLA_DIAG, LANES)
    k3 = k.reshape(G, GLA_DIAG, LANES)
    v3 = v.reshape(G, GLA_DIAG, 2 * DV_GLA)
    sub = lax.broadcasted_iota(jnp.int32, (G, GLA_DIAG, LANES), 1)
    ind = ind_ref[...]
    o = jnp.zeros((C, 2 * DV_GLA), F32)
    for j in range(GLA_DIAG):
        e = jnp.exp(jnp.where(sub >= j, b3 - _bcast_row(b3, j), neg_inf))
        pj = (q3 * _bcast_row(k3, j) * e).reshape(C, LANES).astype(BF16)
        rj = _dot(pj, ind)
        o = o + rj * _bcast_row(v3, j).reshape(C, 2 * DV_GLA)

    ts_xor = (lax.broadcasted_iota(jnp.int32, (C, C), 0)
              ^ lax.broadcasted_iota(jnp.int32, (C, C), 1))
    tok = lax.broadcasted_iota(jnp.int32, (C, LANES), 0)
    a0 = jnp.zeros((C, C), F32)
    a1 = jnp.zeros((C, C), F32)
    m = GLA_DIAG
    while m < C:
        P = C // (2 * m)
        bmid = _bcast_row(b.reshape(P, 2 * m, LANES), m).reshape(C, LANES)
        late = (tok & m) != 0
        qt = q * jnp.exp(jnp.where(late, b - bmid, neg_inf))
        kt = (k * jnp.exp(jnp.where(late, neg_inf, bmid - b))).astype(BF16)
        split_here = (ts_xor // m) == 1
        x0 = _dot_nt(jnp.where(head0, qt, 0.0).astype(BF16), kt)
        x1 = _dot_nt(jnp.where(head0, 0.0, qt).astype(BF16), kt)
        a0 = jnp.where(split_here, x0, a0)
        a1 = jnp.where(split_here, x1, a1)
        m *= 2
    o_intra = jnp.concatenate([_dot(a0.astype(BF16), vb[:, :DV_GLA]),
                               _dot(a1.astype(BF16), vb[:, DV_GLA:])], axis=1)

    st = st_ref[...]
    stb = st.astype(BF16)
    qe = q * jnp.exp(b)
    o_inter = jnp.concatenate(
        [_dot_nt(jnp.where(head0, qe, 0.0).astype(BF16), stb[:DV_GLA]),
         _dot_nt(jnp.where(head0, 0.0, qe).astype(BF16), stb[DV_GLA:])], axis=1)
    o = o + o_intra + o_inter

    b_last = b[C - 1:C, :]
    kd = (k * jnp.exp(b_last - b)).astype(BF16)
    kv0 = _dot(vb[:, :DV_GLA].T, kd)
    kv1 = _dot(vb[:, DV_GLA:].T, kd)
    head0_s = lax.broadcasted_iota(jnp.int32, (DV_GLA, LANES), 1) < DK_GLA
    dec = jnp.exp(b_last)
    st_new = jnp.concatenate([st[:DV_GLA] * dec + jnp.where(head0_s, kv0, 0.0),
                              st[DV_GLA:] * dec + jnp.where(head0_s, 0.0, kv1)], axis=0)
    st_ref[...] = st_new

    gate = gate_ref[...]
    gn = gn_ref[...]
    outs = []
    for hh in range(2):
        oh = o[:, hh * DV_GLA:(hh + 1) * DV_GLA]
        gh = gate[:, hh * DV_GLA:(hh + 1) * DV_GLA]
        r = lax.rsqrt(jnp.mean(oh * oh, axis=-1, keepdims=True) + EPS)
        outs.append(((oh * r) * gn) * (gh * (1.0 / (1.0 + jnp.exp(-gh)))))
    o_ref[...] = jnp.concatenate(outs, axis=1).astype(o_ref.dtype)

    @pl.when(c == pl.num_programs(2) - 1)
    def _():
        s0 = st_new[:DV_GLA].T
        s1 = st_new[DV_GLA:].T
        s_ref[0, 0] = s0[:DK_GLA]
        s_ref[0, 1] = s1[DK_GLA:]


def _gla_prompt(qg, kg, la, vg, gg, wts, batch, seq):
    C = GLA_C
    assert seq % C == 0
    nc = seq // C
    npair = H_GLA // 2
    qk_spec = pl.BlockSpec((C, LANES), lambda b, p, c: (b * nc + c, p))
    v_spec = pl.BlockSpec((C, 2 * DV_GLA), lambda b, p, c: (b * nc + c, p))
    return pl.pallas_call(
        _gla_prompt_kernel,
        grid=(batch, npair, nc),
        in_specs=[qk_spec, qk_spec, qk_spec, v_spec, v_spec,
                  _const_spec((C, C)), _const_spec((LANES, 2 * DV_GLA)), _const_spec((1, DV_GLA))],
        out_specs=[v_spec, pl.BlockSpec((1, 2, DK_GLA, DV_GLA), lambda b, p, c: (b, p, 0, 0))],
        out_shape=[jax.ShapeDtypeStruct((batch * seq, GLA_V_W), BF16),
                   jax.ShapeDtypeStruct((batch, H_GLA, DK_GLA, DV_GLA), F32)],
        scratch_shapes=[pltpu.VMEM((2 * DV_GLA, LANES), F32)],
        compiler_params=_cparams(("parallel", "parallel", "arbitrary")),
        name="gla_prompt",
    )(qg, kg, la, vg, gg, wts['tri'], wts['ind'], wts['gla_g'])


def _gla_step_kernel(q_ref, k_ref, la_ref, v_ref, gate_ref, s_ref, gn_ref, o_ref, so_ref):
    b = pl.program_id(0)
    nb = q_ref.shape[0]

    def column(ref):
        x = ref[...]
        pad = jnp.zeros((LANES - nb, x.shape[1]), F32)
        xt = jnp.concatenate([x, pad], axis=0).T
        lane = lax.broadcasted_iota(jnp.int32, xt.shape, 1)
        return jnp.sum(jnp.where(lane == b, xt, 0.0), axis=1, keepdims=True)

    qc = column(q_ref)
    kc = column(k_ref)
    ac = jnp.exp(column(la_ref))
    v = v_ref[0]
    gate = gate_ref[0]
    gn = gn_ref[...]
    outs = []
    for h in range(H_GLA):
        sl = slice(h * DK_GLA, (h + 1) * DK_GLA)
        vs = slice(h * DV_GLA, (h + 1) * DV_GLA)
        s_new = s_ref[0, h] * ac[sl] + kc[sl] * v[:, vs]
        so_ref[0, h] = s_new
        oh = jnp.sum(qc[sl] * s_new, axis=0, keepdims=True)
        r = lax.rsqrt(jnp.mean(oh * oh, axis=-1, keepdims=True) + EPS)
        gh = gate[:, vs]
        outs.append(((oh * r) * gn) * (gh * (1.0 / (1.0 + jnp.exp(-gh)))))
    o_ref[0] = jnp.concatenate(outs, axis=1)


def _gla_step(qg, kg, la, vg, gg, state, wts):
    nb = qg.shape[0]
    assert nb <= LANES
    full = _const_spec((nb, GLA_QK_W))
    row3 = pl.BlockSpec((1, 1, GLA_V_W), lambda b: (b, 0, 0))
    st_spec = pl.BlockSpec((1, H_GLA, DK_GLA, DV_GLA), lambda b: (b, 0, 0, 0))
    o, s = pl.pallas_call(
        _gla_step_kernel,
        grid=(nb,),
        in_specs=[full, full, full, row3, row3, st_spec, _const_spec((1, DV_GLA))],
        out_specs=[row3, st_spec],
        out_shape=[jax.ShapeDtypeStruct((nb, 1, GLA_V_W), F32),
                   jax.ShapeDtypeStruct(state.shape, F32)],
        compiler_params=_cparams(("parallel",)),
        name="gla_step",
    )(qg, kg, la, vg.reshape(nb, 1, GLA_V_W), gg.reshape(nb, 1, GLA_V_W), state, wts['gla_g'])
    return o.reshape(nb, GLA_V_W), s


def _diff_finish(o1, o2, lam, dn_g):
    o = o1 - lam * o2
    r = lax.rsqrt(jnp.mean(o * o, axis=-1, keepdims=True) + EPS)
    return ((o * r) * dn_g) * (1.0 - LAMBDA_INIT)


def _decode_init(q_ref, qbd_ref, m_ref, l_ref, acc_ref):
    R = 2 * H_DIFF
    rowi = lax.broadcasted_iota(jnp.int32, (R, DIFF_QK_W), 0)
    lanei = lax.broadcasted_iota(jnp.int32, (R, DIFF_QK_W), 1)
    qb = jnp.broadcast_to(q_ref[0], (R, DIFF_QK_W))
    qbd_ref[...] = jnp.where(lanei // DH_DIFF == rowi, qb, 0.0).astype(BF16)
    m_ref[...] = jnp.full_like(m_ref, NEG_BIG)
    l_ref[...] = jnp.zeros_like(l_ref)
    acc_ref[...] = jnp.zeros_like(acc_ref)


def _decode_pages(k_refs, v_refs, qbd_ref, m_ref, l_ref, acc_ref):
    qbd = qbd_ref[...]
    s = jnp.concatenate([_dot(qbd, kr[...].astype(BF16)) for kr in k_refs], axis=1)
    m_old = m_ref[...]
    m_new = jnp.maximum(m_old, jnp.max(s, axis=-1, keepdims=True))
    alpha = jnp.exp2(m_old - m_new)
    p = jnp.exp2(s - m_new)
    l_ref[...] = alpha * l_ref[...] + jnp.sum(p, axis=-1, keepdims=True)
    pb = p.astype(BF16)
    acc_ref[...] = alpha * acc_ref[...]
    for i, vr in enumerate(v_refs):
        pi = pb[:, i * LANES:(i + 1) * LANES]
        for h in range(H_DIFF):
            vh = vr[pl.ds(h, LANES, stride=H_DIFF), :]
            acc_ref[:, h * DV_DIFF:(h + 1) * DV_DIFF] += _dot(pi, vh.astype(BF16))
    m_ref[...] = m_new


def _decode_finish(lam, kn_ref, vn_ref, dn_ref, o_ref, qbd_ref, m_ref, l_ref, acc_ref):
    R = 2 * H_DIFF
    rowi = lax.broadcasted_iota(jnp.int32, (R, DIFF_QK_W), 0)
    lanei = lax.broadcasted_iota(jnp.int32, (R, DIFF_QK_W), 1)
    m_old = m_ref[...]
    s_n = jnp.sum(qbd_ref[...].astype(F32) * kn_ref[0], axis=-1, keepdims=True)
    m_f = jnp.maximum(m_old, s_n)
    a_f = jnp.exp2(m_old - m_f)
    p_n = jnp.exp2(s_n - m_f)
    l_f = a_f * l_ref[...] + p_n
    acc_f = a_f * acc_ref[...] + p_n * vn_ref[0]
    coef = jnp.where((rowi & 1) == 0, 1.0, -lam)
    w = jnp.where(lanei // DV_DIFF == rowi // 2, (acc_f / l_f) * coef, 0.0)
    o = jnp.sum(w, axis=0, keepdims=True)
    dn = dn_ref[...]
    outs = []
    for h in range(H_DIFF):
        oh = o[:, h * DV_DIFF:(h + 1) * DV_DIFF]
        outs.append(_diff_finish(oh, jnp.zeros_like(oh), lam, dn))
    o_ref[0] = jnp.concatenate(outs, axis=1)


def _page_copies(pt_ref, ck_hbm, cv_hbm, kbuf, vbuf, sem, step, slot, pps):
    copies = []
    for k in range(pps):
        pg = pt_ref[step * pps + k]
        copies.append(pltpu.make_async_copy(ck_hbm.at[pg], kbuf.at[slot, k], sem.at[0, slot]))
        copies.append(pltpu.make_async_copy(cv_hbm.at[pg], vbuf.at[slot, k], sem.at[1, slot]))
    return copies


def _attention_kernel(pt_ref, qt_ref, k_ref, vt_ref, lq1, lk1, lq2, lk2, dn_ref, dnc_ref,
                      dq_ref, dkn_ref, dvn_ref, ck_hbm, cv_hbm, o_ref, do_ref,
                      m_ref, acc_ref, sa_ref, sb_ref, qbd_ref, dm_ref, dl_ref, dacc_ref,
                      kbuf, vbuf, sem, *, n_pages_step, n_chunks):
    qi = pl.program_id(2)
    lam = _lambda_full(lq1[...], lk1[...], lq2[...], lk2[...])
    n_steps = pl.num_programs(0) * pl.num_programs(1) * pl.num_programs(2)
    step = (pl.program_id(0) * pl.num_programs(1) + pl.program_id(1)) * pl.num_programs(2) + qi
    chunk = lax.rem(step, n_chunks)
    slot = lax.rem(step, 2)

    @pl.when(step == 0)
    def _():
        for cp in _page_copies(pt_ref, ck_hbm, cv_hbm, kbuf, vbuf, sem, 0, 0, n_pages_step):
            cp.start()

    @pl.when(step + 1 < n_steps)
    def _():
        for cp in _page_copies(pt_ref, ck_hbm, cv_hbm, kbuf, vbuf, sem, step + 1, 1 - slot,
                               n_pages_step):
            cp.start()

    @pl.when(chunk == 0)
    def _():
        _decode_init(dq_ref, qbd_ref, dm_ref, dl_ref, dacc_ref)

    for cp in _page_copies(pt_ref, ck_hbm, cv_hbm, kbuf, vbuf, sem, step, slot, n_pages_step):
        cp.wait()
    _decode_pages([kbuf.at[slot, k] for k in range(n_pages_step)],
                  [vbuf.at[slot, k] for k in range(n_pages_step)],
                  qbd_ref, dm_ref, dl_ref, dacc_ref)

    T = qt_ref.shape[2]
    qt = qt_ref[0]
    dim = lax.broadcasted_iota(jnp.int32, (LANES, T), 0)
    zero = jnp.zeros_like(qt)
    qq = jnp.concatenate([jnp.where(dim < DH_DIFF, qt, zero),
                          jnp.where(dim < DH_DIFF, zero, qt)], axis=1)
    ones = jnp.ones((ONES_ROWS, T), BF16)
    key_i = lax.broadcasted_iota(jnp.int32, (T, 2 * T), 0)
    qry_i = lax.broadcasted_iota(jnp.int32, (T, 2 * T), 1) & (T - 1)
    causal = key_i <= qry_i

    def scores(tile, s_ref):
        s_ref[...] = _dot(k_ref[pl.ds(pl.multiple_of(tile * T, T), T), :], qq)

    def update(tile, s_ref, masked):
        v1 = jnp.concatenate([vt_ref[0, :, pl.ds(pl.multiple_of(tile * T, T), T)], ones], axis=0)
        s = s_ref[...]
        if masked:
            s = jnp.where(causal, s, -jnp.inf)
        m_old = m_ref[...]
        m_new = jnp.maximum(m_old, jnp.max(s, axis=0, keepdims=True))
        alpha = jnp.exp2(m_old - m_new)
        p = jnp.exp2(s - m_new)
        acc_ref[...] = alpha * acc_ref[...] + _dot(v1, p.astype(BF16))
        m_ref[...] = m_new

    m_ref[...] = jnp.full_like(m_ref, -jnp.inf)
    acc_ref[...] = jnp.zeros_like(acc_ref)
    scores(0, sa_ref)

    def body(j, carry):
        t = 2 * j
        scores(t + 1, sb_ref)
        update(t, sa_ref, False)
        scores(t + 2, sa_ref)
        update(t + 1, sb_ref, False)
        return carry

    lax.fori_loop(0, lax.shift_right_logical(qi, 1), body, 0)

    @pl.when((qi & 1) == 0)
    def _():
        update(qi, sa_ref, True)

    @pl.when((qi & 1) == 1)
    def _():
        scores(qi, sb_ref)
        update(qi - 1, sa_ref, False)
        update(qi, sb_ref, True)

    acc = acc_ref[...]
    on = acc[:DV_DIFF] / acc[DV_DIFF:DV_DIFF + 1]
    o = on[:, :T] - lam * on[:, T:]
    r = lax.rsqrt(jnp.mean(o * o, axis=0, keepdims=True) + EPS)
    o = ((o * r) * dnc_ref[...]) * (1.0 - LAMBDA_INIT)
    o_ref[...] = o.T.astype(o_ref.dtype)

    @pl.when(chunk == n_chunks - 1)
    def _():
        _decode_finish(lam, dkn_ref, dvn_ref, dn_ref, do_ref, qbd_ref, dm_ref, dl_ref, dacc_ref)


def _attention(qdt, kd, vdt, qd_s, kd_s, vd_s, cache_kt, cache_vr, page_table, wts, batch, seq):
    T = ATT_T
    assert seq % T == 0 and (T & (T - 1)) == 0
    nq = seq // T
    nb, n_pages = page_table.shape
    assert cache_kt.shape[2] == LANES
    n_steps = batch * H_DIFF * nq
    assert (nb * n_pages) % n_steps == 0
    pps = (nb * n_pages) // n_steps
    assert n_pages % pps == 0
    n_chunks = n_pages // pps

    def step_of(b, h, i):
        return (b * H_DIFF + h) * nq + i

    o_spec = pl.BlockSpec((T, LANES), lambda b, h, i, pt: (b * nq + i, h))
    qt_spec = pl.BlockSpec((1, LANES, T), lambda b, h, i, pt: (b, h, i))
    k_spec = pl.BlockSpec((seq, LANES), lambda b, h, i, pt: (b, h))
    vt_spec = pl.BlockSpec((1, LANES, seq), lambda b, h, i, pt: (b, h, 0))
    lam_spec = pl.BlockSpec((1, DH_DIFF), lambda b, h, i, pt: (0, 0))
    row3 = pl.BlockSpec((1, 1, DIFF_QK_W), lambda b, h, i, pt: (step_of(b, h, i) // n_chunks, 0, 0))

    grid_spec = pltpu.PrefetchScalarGridSpec(
        num_scalar_prefetch=1,
        grid=(batch, H_DIFF, nq),
        in_specs=[qt_spec, k_spec, vt_spec, lam_spec, lam_spec, lam_spec, lam_spec,
                  pl.BlockSpec((1, DV_DIFF), lambda b, h, i, pt: (0, 0)),
                  pl.BlockSpec((DV_DIFF, 1), lambda b, h, i, pt: (0, 0)), row3, row3, row3,
                  pl.BlockSpec(memory_space=pl.ANY), pl.BlockSpec(memory_space=pl.ANY)],
        out_specs=[o_spec, row3],
        scratch_shapes=[pltpu.VMEM((1, 2 * T), F32), pltpu.VMEM((DV_DIFF + ONES_ROWS, 2 * T), F32),
                        pltpu.VMEM((T, 2 * T), F32), pltpu.VMEM((T, 2 * T), F32),
                        pltpu.VMEM((2 * H_DIFF, DIFF_QK_W), BF16),
                        pltpu.VMEM((2 * H_DIFF, 1), F32), pltpu.VMEM((2 * H_DIFF, 1), F32),
                        pltpu.VMEM((2 * H_DIFF, DIFF_V_W), F32),
                        pltpu.VMEM((2, pps, DIFF_QK_W, LANES), F32),
                        pltpu.VMEM((2, pps, DIFF_QK_W, LANES), F32),
                        pltpu.SemaphoreType.DMA((2, 2))],
    )
    o, o_s = pl.pallas_call(
        functools.partial(_attention_kernel, n_pages_step=pps, n_chunks=n_chunks),
        grid_spec=grid_spec,
        out_shape=[jax.ShapeDtypeStruct((batch * seq, DIFF_V_W), BF16),
                   jax.ShapeDtypeStruct((nb, 1, DIFF_V_W), F32)],
        compiler_params=_cparams(("arbitrary", "arbitrary", "arbitrary")),
        name="attention",
    )(page_table.reshape(-1), qdt, kd, vdt,
      wts['lq1'], wts['lk1'], wts['lq2'], wts['lk2'], wts['dn_g'], wts['dn_g'].reshape(DV_DIFF, 1),
      qd_s.astype(F32).reshape(nb, 1, DIFF_QK_W), kd_s.reshape(nb, 1, DIFF_QK_W),
      vd_s.reshape(nb, 1, DIFF_V_W), cache_kt, cache_vr)
    return o, o_s.reshape(nb, DIFF_V_W)


def _out_ffn_kernel(x_ref, mg_ref, md_ref, wo_ref, g_ref, wup_ref, wdn_ref, y_ref, *, fc):
    half = GLA_V_W
    x1 = (x_ref[...] + _dot(mg_ref[...].astype(BF16), wo_ref[0:half, :])
          + _dot(md_ref[...].astype(BF16), wo_ref[half:, :]))
    r = lax.rsqrt(jnp.mean(x1 * x1, axis=-1, keepdims=True) + EPS)
    h2 = ((x1 * r) * g_ref[...]).astype(BF16)
    y_ref[...] = x1
    for f in range(D_FF // fc):
        u = jnp.maximum(_dot(h2, wup_ref[:, f * fc:(f + 1) * fc]), 0.0)
        y_ref[...] += _dot((u * u).astype(BF16), wdn_ref[f * fc:(f + 1) * fc, :])


def _out_ffn(x, mix_g, mix_d, wts, tm, fc=512):
    n = x.shape[0]
    assert n % tm == 0
    row = lambda w: pl.BlockSpec((tm, w), lambda i: (i, 0))
    consts = [wts['w_out'], wts['g_ffn'], wts['w_up'], wts['w_down']]
    const_specs = [_resident_spec(c.shape) for c in consts]
    return pl.pallas_call(
        functools.partial(_out_ffn_kernel, fc=fc),
        grid=(n // tm,),
        in_specs=[row(D_MODEL), row(GLA_V_W), row(DIFF_V_W)] + const_specs,
        out_specs=row(D_MODEL),
        out_shape=jax.ShapeDtypeStruct((n, D_MODEL), F32),
        compiler_params=_cparams(("parallel",)),
        name="out_ffn",
    )(x, mix_g, mix_d, *consts)


def _prep_weights(norm_mix_g, w_in, w_gk2, b_gk, gla_norm_g, q_norm_g, k_norm_g,
                  lambda_q1, lambda_k1, lambda_q2, lambda_k2, diff_norm_g, w_out,
                  norm_ffn_g, w_up, w_down):
    w = w_in[0]
    blk = np.arange(2 * LANES) // DH_DIFF
    ind = (np.arange(LANES)[:, None] // DK_GLA) == (np.arange(2 * DV_GLA)[None, :] // DV_GLA)
    return {
        'g_mix': norm_mix_g[0][None, :],
        'w_gla': w[:, :OFF_GLR].astype(BF16),
        'w_glr': jnp.pad(w[:, OFF_GLR:OFF_DIFF], ((0, 0), (0, LANES - GK_RANK))).astype(BF16),
        'w_gk2': jnp.pad(w_gk2[0], ((0, LANES - GK_RANK), (0, 0))).astype(BF16),
        'b_gk': b_gk[0][None, :],
        'w_diff': w[:, OFF_DIFF:].astype(BF16),
        'qn_g': jnp.tile(q_norm_g[0], 2 * H_DIFF)[None, :],
        'kn_g': jnp.tile(k_norm_g[0], 2 * H_DIFF)[None, :],
        'ones_blk': jnp.asarray(blk[:, None] == blk[None, :], BF16),
        'tri': jnp.asarray(np.tril(np.ones((GLA_C, GLA_C))), BF16),
        'ind': jnp.asarray(ind, BF16),
        'gla_g': gla_norm_g[0][None, :],
        'lq1': lambda_q1[0][None, :], 'lk1': lambda_k1[0][None, :],
        'lq2': lambda_q2[0][None, :], 'lk2': lambda_k2[0][None, :],
        'dn_g': diff_norm_g[0][None, :],
        'w_out': w_out[0].astype(BF16),
        'g_ffn': norm_ffn_g[0][None, :],
        'w_up': w_up[0].astype(BF16),
        'w_down': w_down[0].astype(BF16),
    }


def kernel(x_prompt, x_sample, cache_k, cache_v, page_table, state_gla, norm_mix_g, w_in, w_gk2, b_gk, gla_norm_g, q_norm_g, k_norm_g, lambda_q1, lambda_k1, lambda_q2, lambda_k2, diff_norm_g, w_out, norm_ffn_g, w_up, w_down):
    assert w_in.shape[0] == 1, "single-layer trunk"
    B, T, D = x_prompt.shape
    Bd, Td, _ = x_sample.shape
    assert Td == 1
    wts = _prep_weights(norm_mix_g, w_in, w_gk2, b_gk, gla_norm_g, q_norm_g, k_norm_g,
                        lambda_q1, lambda_k1, lambda_q2, lambda_k2, diff_norm_g, w_out,
                        norm_ffn_g, w_up, w_down)

    xp = x_prompt.reshape(B * T, D)
    xs = x_sample.reshape(Bd, D)
    qg, kg, vg, gg, la, qd_t, kdf_t, kdb, vdf_r, vdb_t = _in_proj(xp, wts, tm=TOK_TILE, seq=T)
    qg_s, kg_s, vg_s, gg_s, la_s, qd_s, kdf_s, _, vdf_s, _ = _in_proj(xs, wts, tm=Bd)

    mix_g, s_prompt = _gla_prompt(qg, kg, la, vg, gg, wts, B, T)
    mixg_s, s_sample = _gla_step(qg_s, kg_s, la_s, vg_s, gg_s, state_gla[0], wts)

    n_pool, page = cache_k.shape[1], cache_k.shape[2]
    cache_kt = jnp.transpose(cache_k[0], (0, 2, 3, 4, 1)).reshape(n_pool, DIFF_QK_W, page)
    cache_vr = cache_v[0].reshape(n_pool, page * H_DIFF, DV_DIFF)
    mix_d, mixd_s = _attention(qd_t, kdb, vdb_t, qd_s, kdf_s, vdf_s, cache_kt, cache_vr,
                               page_table, wts, B, T)

    y_prompt = _out_ffn(xp, mix_g, mix_d, wts, tm=TOK_TILE)
    y_sample = _out_ffn(xs, mixg_s, mixd_s, wts, tm=Bd)
    k_prompt = jnp.transpose(kdf_t.reshape(1, B, H_DIFF, 2, DH_DIFF, T), (0, 1, 5, 2, 3, 4))
    v_prompt = vdf_r.reshape(1, B, T, H_DIFF, DV_DIFF)

    return (y_prompt.reshape(B, T, D),
            y_sample.reshape(Bd, Td, D),
            k_prompt,
            v_prompt,
            s_prompt[None],
            kdf_s.reshape(1, Bd, Td, H_DIFF, 2, DH_DIFF),
            vdf_s.reshape(1, Bd, Td, H_DIFF, DV_DIFF),
            s_sample[None])
```

```python
import functools
import math

import jax
import jax.numpy as jnp
import numpy as np
from jax import lax
from jax.experimental import pallas as pl
from jax.experimental.pallas import tpu as pltpu

F32 = jnp.float32
BF16 = jnp.bfloat16

LANES = 128
SUBLANES = 8
VMEM_LIMIT = 56 * 1024 * 1024

D_MODEL = 1024
H_GLA = 4
DK_GLA = 64
DV_GLA = 128
GK_RANK = 16
GK_NORMALIZER = 16.0
H_DIFF = 4
DH_DIFF = 64
DV_DIFF = 128
D_FF = 4 * D_MODEL
EPS = 1e-6
LAMBDA_INIT = 0.8 - 0.6 * math.exp(-0.3 * 0)

GLA_QK_W = H_GLA * DK_GLA
GLA_V_W = H_GLA * DV_GLA
DIFF_QK_W = H_DIFF * 2 * DH_DIFF
DIFF_V_W = H_DIFF * DV_DIFF
OFF_GLR = 2 * GLA_QK_W + 2 * GLA_V_W
OFF_DIFF = OFF_GLR + GK_RANK

LOG2E = 1.4426950408889634
NEG_BIG = -1e30

GLA_C = 256
GLA_DIAG = 8
ATT_T = 512
ONES_ROWS = 16
TOK_TILE = 512


def _cparams(sem):
    return pltpu.CompilerParams(dimension_semantics=sem, vmem_limit_bytes=VMEM_LIMIT)


def _const_spec(shape):
    nd = len(shape)
    return pl.BlockSpec(shape, lambda *_: (0,) * nd)


def _resident_spec(shape):
    nd = len(shape)
    return pl.BlockSpec(shape, lambda *_: (0,) * nd, pipeline_mode=pl.Buffered(1))


def _dot(a, b):
    return jnp.dot(a, b, preferred_element_type=F32)


def _dot_nt(a, b):
    return lax.dot_general(a, b, (((1,), (1,)), ((), ())), preferred_element_type=F32)


def _lambda_full(lq1, lk1, lq2, lk2):
    a = jnp.sum(lq1 * lk1, axis=-1, keepdims=True)
    b = jnp.sum(lq2 * lk2, axis=-1, keepdims=True)
    return jnp.exp(a) - jnp.exp(b) + LAMBDA_INIT


def _group_rms(z, ones_blk, group):
    outs = []
    w = ones_blk.shape[0]
    for c in range(z.shape[1] // w):
        zc = z[:, c * w:(c + 1) * w]
        outs.append(_dot((zc * zc).astype(BF16), ones_blk))
    ss = jnp.concatenate(outs, axis=1)
    return lax.rsqrt(ss * (1.0 / group) + EPS)


def _in_proj_kernel(x_ref, g_ref, wgla_ref, wglr_ref, wgk2_ref, bgk_ref, wdiff_ref,
                    qng_ref, kng_ref, ones_ref,
                    qg_ref, kg_ref, vg_ref, gg_ref, la_ref,
                    qd_ref, kdf_ref, kdb_ref, vdf_ref, vdb_ref, *, cache_layout):
    x = x_ref[...]
    r = lax.rsqrt(jnp.mean(x * x, axis=-1, keepdims=True) + EPS)
    h = ((x * r) * g_ref[...]).astype(BF16)

    qg_ref[...] = _dot(h, wgla_ref[:, 0:GLA_QK_W]) * (DK_GLA ** -0.5)
    kg_ref[...] = _dot(h, wgla_ref[:, GLA_QK_W:2 * GLA_QK_W])
    vg_ref[...] = _dot(h, wgla_ref[:, 2 * GLA_QK_W:2 * GLA_QK_W + GLA_V_W])
    gg_ref[...] = _dot(h, wgla_ref[:, 2 * GLA_QK_W + GLA_V_W:])

    glr = _dot(h, wglr_ref[...])
    gl = _dot(glr.astype(BF16), wgk2_ref[...]) + bgk_ref[...]
    log_sig = jnp.minimum(gl, 0.0) - jnp.log1p(jnp.exp(-jnp.abs(gl)))
    la_ref[...] = log_sig / GK_NORMALIZER

    ones_blk = ones_ref[...]
    dq = _dot(h, wdiff_ref[:, 0:DIFF_QK_W])
    qn = (dq * _group_rms(dq, ones_blk, DH_DIFF)) * qng_ref[...]
    qs = qn * (DH_DIFF ** -0.5 * LOG2E)
    dk = _dot(h, wdiff_ref[:, DIFF_QK_W:2 * DIFF_QK_W])
    kn = (dk * _group_rms(dk, ones_blk, DH_DIFF)) * kng_ref[...]
    dv = _dot(h, wdiff_ref[:, 2 * DIFF_QK_W:])
    kdb_ref[...] = kn.astype(BF16)
    if cache_layout:
        qd_ref[0] = qs.T.astype(BF16)
        kdf_ref[0] = kn.T
        vdb_ref[0] = dv.T.astype(BF16)
        tm = dv.shape[0]
        for hh in range(H_DIFF):
            vdf_ref[pl.ds(hh, tm, stride=H_DIFF), :] = dv[:, hh * DV_DIFF:(hh + 1) * DV_DIFF]
    else:
        qd_ref[...] = qs.astype(BF16)
        kdf_ref[...] = kn
        vdf_ref[...] = dv
        vdb_ref[...] = dv.astype(BF16)


def _in_proj(x, wts, tm, seq=None):
    n = x.shape[0]
    assert n % tm == 0
    row = lambda w: pl.BlockSpec((tm, w), lambda i: (i, 0))
    out_w = [(GLA_QK_W, F32), (GLA_QK_W, F32), (GLA_V_W, F32), (GLA_V_W, F32), (GLA_QK_W, F32),
             (DIFF_QK_W, BF16), (DIFF_QK_W, F32), (DIFF_QK_W, BF16), (DIFF_V_W, F32), (DIFF_V_W, BF16)]
    out_specs = [row(w) for w, _ in out_w]
    out_shape = [jax.ShapeDtypeStruct((n, w), dt) for w, dt in out_w]
    if seq is not None:
        assert seq % tm == 0 and n % seq == 0
        per = seq // tm
        kt_spec = pl.BlockSpec((1, DIFF_QK_W, tm), lambda i: (i // per, 0, i % per))
        out_specs[5] = out_specs[6] = out_specs[9] = kt_spec
        out_shape[5] = jax.ShapeDtypeStruct((n // seq, DIFF_QK_W, seq), BF16)
        out_shape[6] = jax.ShapeDtypeStruct((n // seq, DIFF_QK_W, seq), F32)
        out_shape[9] = jax.ShapeDtypeStruct((n // seq, DIFF_V_W, seq), BF16)
        out_specs[8] = pl.BlockSpec((tm * H_DIFF, DV_DIFF), lambda i: (i, 0))
        out_shape[8] = jax.ShapeDtypeStruct((n * H_DIFF, DV_DIFF), F32)
    consts = [wts['g_mix'], wts['w_gla'], wts['w_glr'], wts['w_gk2'], wts['b_gk'], wts['w_diff'],
              wts['qn_g'], wts['kn_g'], wts['ones_blk']]
    return pl.pallas_call(
        functools.partial(_in_proj_kernel, cache_layout=seq is not None),
        grid=(n // tm,),
        in_specs=[row(D_MODEL)] + [_resident_spec(c.shape) for c in consts],
        out_specs=out_specs,
        out_shape=out_shape,
        compiler_params=_cparams(("parallel",)),
        name="in_proj",
    )(x, *consts)


def _bcast_row(x3, j):
    return jnp.broadcast_to(x3[:, j:j + 1, :], x3.shape)


def _gla_prompt_kernel(q_ref, k_ref, la_ref, v_ref, gate_ref, tri_ref, ind_ref, gn_ref,
                       o_ref, s_ref, st_ref):
    c = pl.program_id(2)
    C = q_ref.shape[0]
    neg_inf = -jnp.inf

    @pl.when(c == 0)
    def _():
        st_ref[...] = jnp.zeros_like(st_ref)

    la = la_ref[...]
    tri = tri_ref[...]
    t1 = la.astype(BF16)
    r1 = la - t1.astype(F32)
    t2 = r1.astype(BF16)
    t3 = (r1 - t2.astype(F32)).astype(BF16)
    b = _dot(tri, t1) + _dot(tri, t2) + _dot(tri, t3)

    q = q_ref[...]
    k = k_ref[...]
    v = v_ref[...]
    vb = v.astype(BF16)
    lane = lax.broadcasted_iota(jnp.int32, (C, LANES), 1)
    head0 = lane < DK_GLA

    G = C // GLA_DIAG
    b3 = b.reshape(G, GLA_DIAG, LANES)
    q3 = q.reshape(G, GLA_DIAG, LANES)
    k3 = k.reshape(G, GLA_DIAG, LANES)
    v3 = v.reshape(G, GLA_DIAG, 2 * DV_GLA)
    sub = lax.broadcasted_iota(jnp.int32, (G, GLA_DIAG, LANES), 1)
    ind = ind_ref[...]
    o = jnp.zeros((C, 2 * DV_GLA), F32)
    for j in range(GLA_DIAG):
        e = jnp.exp(jnp.where(sub >= j, b3 - _bcast_row(b3, j), neg_inf))
        pj = (q3 * _bcast_row(k3, j) * e).reshape(C, LANES).astype(BF16)
        rj = _dot(pj, ind)
        o = o + rj * _bcast_row(v3, j).reshape(C, 2 * DV_GLA)

    ts_xor = (lax.broadcasted_iota(jnp.int32, (C, C), 0)
              ^ lax.broadcasted_iota(jnp.int32, (C, C), 1))
    tok = lax.broadcasted_iota(jnp.int32, (C, LANES), 0)
    a0 = jnp.zeros((C, C), F32)
    a1 = jnp.zeros((C, C), F32)
    m = GLA_DIAG
    while m < C:
        P = C // (2 * m)
        bmid = _bcast_row(b.reshape(P, 2 * m, LANES), m).reshape(C, LANES)
        late = (tok & m) != 0
        qt = q * jnp.exp(jnp.where(late, b - bmid, neg_inf))
        kt = (k * jnp.exp(jnp.where(late, neg_inf, bmid - b))).astype(BF16)
        split_here = (ts_xor // m) == 1
        x0 = _dot_nt(jnp.where(head0, qt, 0.0).astype(BF16), kt)
        x1 = _dot_nt(jnp.where(head0, 0.0, qt).astype(BF16), kt)
        a0 = jnp.where(split_here, x0, a0)
        a1 = jnp.where(split_here, x1, a1)
        m *= 2
    o_intra = jnp.concatenate([_dot(a0.astype(BF16), vb[:, :DV_GLA]),
                               _dot(a1.astype(BF16), vb[:, DV_GLA:])], axis=1)

    st = st_ref[...]
    stb = st.astype(BF16)
    qe = q * jnp.exp(b)
    o_inter = jnp.concatenate(
        [_dot_nt(jnp.where(head0, qe, 0.0).astype(BF16), stb[:DV_GLA]),
         _dot_nt(jnp.where(head0, 0.0, qe).astype(BF16), stb[DV_GLA:])], axis=1)
    o = o + o_intra + o_inter

    b_last = b[C - 1:C, :]
    kd = (k * jnp.exp(b_last - b)).astype(BF16)
    kv0 = _dot(vb[:, :DV_GLA].T, kd)
    kv1 = _dot(vb[:, DV_GLA:].T, kd)
    head0_s = lax.broadcasted_iota(jnp.int32, (DV_GLA, LANES), 1) < DK_GLA
    dec = jnp.exp(b_last)
    st_new = jnp.concatenate([st[:DV_GLA] * dec + jnp.where(head0_s, kv0, 0.0),
                              st[DV_GLA:] * dec + jnp.where(head0_s, 0.0, kv1)], axis=0)
    st_ref[...] = st_new

    gate = gate_ref[...]
    gn = gn_ref[...]
    outs = []
    for hh in range(2):
        oh = o[:, hh * DV_GLA:(hh + 1) * DV_GLA]
        gh = gate[:, hh * DV_GLA:(hh + 1) * DV_GLA]
        r = lax.rsqrt(jnp.mean(oh * oh, axis=-1, keepdims=True) + EPS)
        outs.append(((oh * r) * gn) * (gh * (1.0 / (1.0 + jnp.exp(-gh)))))
    o_ref[...] = jnp.concatenate(outs, axis=1).astype(o_ref.dtype)

    @pl.when(c == pl.num_programs(2) - 1)
    def _():
        s0 = st_new[:DV_GLA].T
        s1 = st_new[DV_GLA:].T
        s_ref[0, 0] = s0[:DK_GLA]
        s_ref[0, 1] = s1[DK_GLA:]


def _gla_prompt(qg, kg, la, vg, gg, wts, batch, seq):
    C = GLA_C
    assert seq % C == 0
    nc = seq // C
    npair = H_GLA // 2
    qk_spec = pl.BlockSpec((C, LANES), lambda b, p, c: (b * nc + c, p))
    v_spec = pl.BlockSpec((C, 2 * DV_GLA), lambda b, p, c: (b * nc + c, p))
    return pl.pallas_call(
        _gla_prompt_kernel,
        grid=(batch, npair, nc),
        in_specs=[qk_spec, qk_spec, qk_spec, v_spec, v_spec,
                  _const_spec((C, C)), _const_spec((LANES, 2 * DV_GLA)), _const_spec((1, DV_GLA))],
        out_specs=[v_spec, pl.BlockSpec((1, 2, DK_GLA, DV_GLA), lambda b, p, c: (b, p, 0, 0))],
        out_shape=[jax.ShapeDtypeStruct((batch * seq, GLA_V_W), BF16),
                   jax.ShapeDtypeStruct((batch, H_GLA, DK_GLA, DV_GLA), F32)],
        scratch_shapes=[pltpu.VMEM((2 * DV_GLA, LANES), F32)],
        compiler_params=_cparams(("parallel", "parallel", "arbitrary")),
        name="gla_prompt",
    )(qg, kg, la, vg, gg, wts['tri'], wts['ind'], wts['gla_g'])


def _gla_step_kernel(q_ref, k_ref, la_ref, v_ref, gate_ref, s_ref, gn_ref, o_ref, so_ref):
    b = pl.program_id(0)
    nb = q_ref.shape[0]

    def column(ref):
        x = ref[...]
        pad = jnp.zeros((LANES - nb, x.shape[1]), F32)
        xt = jnp.concatenate([x, pad], axis=0).T
        lane = lax.broadcasted_iota(jnp.int32, xt.shape, 1)
        return jnp.sum(jnp.where(lane == b, xt, 0.0), axis=1, keepdims=True)

    qc = column(q_ref)
    kc = column(k_ref)
    ac = jnp.exp(column(la_ref))
    v = v_ref[0]
    gate = gate_ref[0]
    gn = gn_ref[...]
    outs = []
    for h in range(H_GLA):
        sl = slice(h * DK_GLA, (h + 1) * DK_GLA)
        vs = slice(h * DV_GLA, (h + 1) * DV_GLA)
        s_new = s_ref[0, h] * ac[sl] + kc[sl] * v[:, vs]
        so_ref[0, h] = s_new
        oh = jnp.sum(qc[sl] * s_new, axis=0, keepdims=True)
        r = lax.rsqrt(jnp.mean(oh * oh, axis=-1, keepdims=True) + EPS)
        gh = gate[:, vs]
        outs.append(((oh * r) * gn) * (gh * (1.0 / (1.0 + jnp.exp(-gh)))))
    o_ref[0] = jnp.concatenate(outs, axis=1)


def _gla_step(qg, kg, la, vg, gg, state, wts):
    nb = qg.shape[0]
    assert nb <= LANES
    full = _const_spec((nb, GLA_QK_W))
    row3 = pl.BlockSpec((1, 1, GLA_V_W), lambda b: (b, 0, 0))
    st_spec = pl.BlockSpec((1, H_GLA, DK_GLA, DV_GLA), lambda b: (b, 0, 0, 0))
    o, s = pl.pallas_call(
        _gla_step_kernel,
        grid=(nb,),
        in_specs=[full, full, full, row3, row3, st_spec, _const_spec((1, DV_GLA))],
        out_specs=[row3, st_spec],
        out_shape=[jax.ShapeDtypeStruct((nb, 1, GLA_V_W), F32),
                   jax.ShapeDtypeStruct(state.shape, F32)],
        compiler_params=_cparams(("parallel",)),
        name="gla_step",
    )(qg, kg, la, vg.reshape(nb, 1, GLA_V_W), gg.reshape(nb, 1, GLA_V_W), state, wts['gla_g'])
    return o.reshape(nb, GLA_V_W), s


def _diff_finish(o1, o2, lam, dn_g):
    o = o1 - lam * o2
    r = lax.rsqrt(jnp.mean(o * o, axis=-1, keepdims=True) + EPS)
    return ((o * r) * dn_g) * (1.0 - LAMBDA_INIT)


def _decode_init(q_ref, qbd_ref, m_ref, l_ref, acc_ref):
    R = 2 * H_DIFF
    rowi = lax.broadcasted_iota(jnp.int32, (R, DIFF_QK_W), 0)
    lanei = lax.broadcasted_iota(jnp.int32, (R, DIFF_QK_W), 1)
    qb = jnp.broadcast_to(q_ref[0], (R, DIFF_QK_W))
    qbd_ref[...] = jnp.where(lanei // DH_DIFF == rowi, qb, 0.0).astype(BF16)
    m_ref[...] = jnp.full_like(m_ref, NEG_BIG)
    l_ref[...] = jnp.zeros_like(l_ref)
    acc_ref[...] = jnp.zeros_like(acc_ref)


def _decode_pages(k_refs, v_refs, qbd_ref, m_ref, l_ref, acc_ref):
    qbd = qbd_ref[...]
    s = jnp.concatenate(
        [_dot(qbd, jnp.concatenate([k_refs[i][...], k_refs[i + 1][...]], axis=1).astype(BF16))
         for i in range(0, len(k_refs), 2)], axis=1)
    m_old = m_ref[...]
    m_new = jnp.maximum(m_old, jnp.max(s, axis=-1, keepdims=True))
    alpha = jnp.exp2(m_old - m_new)
    p = jnp.exp2(s - m_new)
    l_ref[...] = alpha * l_ref[...] + jnp.sum(p, axis=-1, keepdims=True)
    pb = p.astype(BF16)
    acc_ref[...] = alpha * acc_ref[...]
    for i in range(0, len(v_refs), 2):
        pi = pb[:, i * LANES:(i + 2) * LANES]
        for h in range(0, H_DIFF, 2):
            vh = jnp.concatenate(
                [jnp.concatenate([vr[pl.ds(hh, LANES, stride=H_DIFF), :] for hh in (h, h + 1)], axis=1)
                 for vr in (v_refs[i], v_refs[i + 1])], axis=0)
            acc_ref[:, h * DV_DIFF:(h + 2) * DV_DIFF] += _dot(pi, vh.astype(BF16))
    m_ref[...] = m_new


def _decode_finish(lam, kn_ref, vn_ref, dn_ref, o_ref, qbd_ref, m_ref, l_ref, acc_ref):
    R = 2 * H_DIFF
    rowi = lax.broadcasted_iota(jnp.int32, (R, DIFF_QK_W), 0)
    lanei = lax.broadcasted_iota(jnp.int32, (R, DIFF_QK_W), 1)
    m_old = m_ref[...]
    s_n = jnp.sum(qbd_ref[...].astype(F32) * kn_ref[0], axis=-1, keepdims=True)
    m_f = jnp.maximum(m_old, s_n)
    a_f = jnp.exp2(m_old - m_f)
    p_n = jnp.exp2(s_n - m_f)
    l_f = a_f * l_ref[...] + p_n
    acc_f = a_f * acc_ref[...] + p_n * vn_ref[0]
    coef = jnp.where((rowi & 1) == 0, 1.0, -lam)
    w = jnp.where(lanei // DV_DIFF == rowi // 2, (acc_f / l_f) * coef, 0.0)
    o = jnp.sum(w, axis=0, keepdims=True)
    dn = dn_ref[...]
    outs = []
    for h in range(H_DIFF):
        oh = o[:, h * DV_DIFF:(h + 1) * DV_DIFF]
        outs.append(_diff_finish(oh, jnp.zeros_like(oh), lam, dn))
    o_ref[0] = jnp.concatenate(outs, axis=1)


def _page_copies(pt_ref, ck_hbm, cv_hbm, kbuf, vbuf, sem, step, slot, pps):
    copies = []
    for k in range(pps):
        pg = pt_ref[step * pps + k]
        copies.append(pltpu.make_async_copy(ck_hbm.at[pg], kbuf.at[slot, k], sem.at[0, slot]))
        copies.append(pltpu.make_async_copy(cv_hbm.at[pg], vbuf.at[slot, k], sem.at[1, slot]))
    return copies


def _attention_kernel(pt_ref, qt_ref, k_ref, vt_ref, lq1, lk1, lq2, lk2, dn_ref, dnc_ref,
                      dq_ref, dkn_ref, dvn_ref, ck_hbm, cv_hbm, o_ref, do_ref,
                      m_ref, acc_ref, sa_ref, sb_ref, qbd_ref, dm_ref, dl_ref, dacc_ref,
                      kbuf, vbuf, sem, *, n_pages_step, n_chunks):
    qi = pl.program_id(2)
    lam = _lambda_full(lq1[...], lk1[...], lq2[...], lk2[...])
    n_steps = pl.num_programs(0) * pl.num_programs(1) * pl.num_programs(2)
    step = (pl.program_id(0) * pl.num_programs(1) + pl.program_id(1)) * pl.num_programs(2) + qi
    chunk = lax.rem(step, n_chunks)
    slot = lax.rem(step, 2)

    @pl.when(step == 0)
    def _():
        for cp in _page_copies(pt_ref, ck_hbm, cv_hbm, kbuf, vbuf, sem, 0, 0, n_pages_step):
            cp.start()

    @pl.when(step + 1 < n_steps)
    def _():
        for cp in _page_copies(pt_ref, ck_hbm, cv_hbm, kbuf, vbuf, sem, step + 1, 1 - slot,
                               n_pages_step):
            cp.start()

    @pl.when(chunk == 0)
    def _():
        _decode_init(dq_ref, qbd_ref, dm_ref, dl_ref, dacc_ref)

    for cp in _page_copies(pt_ref, ck_hbm, cv_hbm, kbuf, vbuf, sem, step, slot, n_pages_step):
        cp.wait()
    _decode_pages([kbuf.at[slot, k] for k in range(n_pages_step)],
                  [vbuf.at[slot, k] for k in range(n_pages_step)],
                  qbd_ref, dm_ref, dl_ref, dacc_ref)

    T = qt_ref.shape[2]
    qt = qt_ref[0]
    dim = lax.broadcasted_iota(jnp.int32, (LANES, T), 0)
    zero = jnp.zeros_like(qt)
    qq = jnp.concatenate([jnp.where(dim < DH_DIFF, qt, zero),
                          jnp.where(dim < DH_DIFF, zero, qt)], axis=1)
    ones = jnp.ones((ONES_ROWS, T), BF16)
    key_i = lax.broadcasted_iota(jnp.int32, (T, 2 * T), 0)
    qry_i = lax.broadcasted_iota(jnp.int32, (T, 2 * T), 1) & (T - 1)
    causal = key_i <= qry_i

    def scores(tile, s_ref):
        s_ref[...] = _dot(k_ref[pl.ds(pl.multiple_of(tile * T, T), T), :], qq)

    def update(tile, s_ref, masked):
        v1 = jnp.concatenate([vt_ref[0, :, pl.ds(pl.multiple_of(tile * T, T), T)], ones], axis=0)
        s = s_ref[...]
        if masked:
            s = jnp.where(causal, s, -jnp.inf)
        m_old = m_ref[...]
        m_new = jnp.maximum(m_old, jnp.max(s, axis=0, keepdims=True))
        alpha = jnp.exp2(m_old - m_new)
        p = jnp.exp2(s - m_new)
        acc_ref[...] = alpha * acc_ref[...] + _dot(v1, p.astype(BF16))
        m_ref[...] = m_new

    m_ref[...] = jnp.full_like(m_ref, -jnp.inf)
    acc_ref[...] = jnp.zeros_like(acc_ref)
    scores(0, sa_ref)

    def body(j, carry):
        t = 2 * j
        scores(t + 1, sb_ref)
        update(t, sa_ref, False)
        scores(t + 2, sa_ref)
        update(t + 1, sb_ref, False)
        return carry

    lax.fori_loop(0, lax.shift_right_logical(qi, 1), body, 0)

    @pl.when((qi & 1) == 0)
    def _():
        update(qi, sa_ref, True)

    @pl.when((qi & 1) == 1)
    def _():
        scores(qi, sb_ref)
        update(qi - 1, sa_ref, False)
        update(qi, sb_ref, True)

    acc = acc_ref[...]
    on = acc[:DV_DIFF] / acc[DV_DIFF:DV_DIFF + 1]
    o = on[:, :T] - lam * on[:, T:]
    r = lax.rsqrt(jnp.mean(o * o, axis=0, keepdims=True) + EPS)
    o = ((o * r) * dnc_ref[...]) * (1.0 - LAMBDA_INIT)
    o_ref[...] = o.T.astype(o_ref.dtype)

    @pl.when(chunk == n_chunks - 1)
    def _():
        _decode_finish(lam, dkn_ref, dvn_ref, dn_ref, do_ref, qbd_ref, dm_ref, dl_ref, dacc_ref)


def _attention(qdt, kd, vdt, qd_s, kd_s, vd_s, cache_kt, cache_vr, page_table, wts, batch, seq):
    T = ATT_T
    assert seq % T == 0 and (T & (T - 1)) == 0
    nq = seq // T
    nb, n_pages = page_table.shape
    assert cache_kt.shape[2] == LANES
    n_steps = batch * H_DIFF * nq
    assert (nb * n_pages) % n_steps == 0
    pps = (nb * n_pages) // n_steps
    assert n_pages % pps == 0 and pps % 2 == 0
    n_chunks = n_pages // pps

    def step_of(b, h, i):
        return (b * H_DIFF + h) * nq + i

    o_spec = pl.BlockSpec((T, LANES), lambda b, h, i, pt: (b * nq + i, h))
    qt_spec = pl.BlockSpec((1, LANES, T), lambda b, h, i, pt: (b, h, i))
    k_spec = pl.BlockSpec((seq, LANES), lambda b, h, i, pt: (b, h))
    vt_spec = pl.BlockSpec((1, LANES, seq), lambda b, h, i, pt: (b, h, 0))
    lam_spec = pl.BlockSpec((1, DH_DIFF), lambda b, h, i, pt: (0, 0))
    row3 = pl.BlockSpec((1, 1, DIFF_QK_W), lambda b, h, i, pt: (step_of(b, h, i) // n_chunks, 0, 0))

    grid_spec = pltpu.PrefetchScalarGridSpec(
        num_scalar_prefetch=1,
        grid=(batch, H_DIFF, nq),
        in_specs=[qt_spec, k_spec, vt_spec, lam_spec, lam_spec, lam_spec, lam_spec,
                  pl.BlockSpec((1, DV_DIFF), lambda b, h, i, pt: (0, 0)),
                  pl.BlockSpec((DV_DIFF, 1), lambda b, h, i, pt: (0, 0)), row3, row3, row3,
                  pl.BlockSpec(memory_space=pl.ANY), pl.BlockSpec(memory_space=pl.ANY)],
        out_specs=[o_spec, row3],
        scratch_shapes=[pltpu.VMEM((1, 2 * T), F32), pltpu.VMEM((DV_DIFF + ONES_ROWS, 2 * T), F32),
                        pltpu.VMEM((T, 2 * T), F32), pltpu.VMEM((T, 2 * T), F32),
                        pltpu.VMEM((2 * H_DIFF, DIFF_QK_W), BF16),
                        pltpu.VMEM((2 * H_DIFF, 1), F32), pltpu.VMEM((2 * H_DIFF, 1), F32),
                        pltpu.VMEM((2 * H_DIFF, DIFF_V_W), F32),
                        pltpu.VMEM((2, pps, DIFF_QK_W, LANES), F32),
                        pltpu.VMEM((2, pps, DIFF_QK_W, LANES), F32),
                        pltpu.SemaphoreType.DMA((2, 2))],
    )
    o, o_s = pl.pallas_call(
        functools.partial(_attention_kernel, n_pages_step=pps, n_chunks=n_chunks),
        grid_spec=grid_spec,
        out_shape=[jax.ShapeDtypeStruct((batch * seq, DIFF_V_W), BF16),
                   jax.ShapeDtypeStruct((nb, 1, DIFF_V_W), F32)],
        compiler_params=_cparams(("arbitrary", "arbitrary", "arbitrary")),
        name="attention",
    )(page_table.reshape(-1), qdt, kd, vdt,
      wts['lq1'], wts['lk1'], wts['lq2'], wts['lk2'], wts['dn_g'], wts['dn_g'].reshape(DV_DIFF, 1),
      qd_s.astype(F32).reshape(nb, 1, DIFF_QK_W), kd_s.reshape(nb, 1, DIFF_QK_W),
      vd_s.reshape(nb, 1, DIFF_V_W), cache_kt, cache_vr)
    return o, o_s.reshape(nb, DIFF_V_W)


def _out_ffn_kernel(x_ref, mg_ref, md_ref, wo_ref, g_ref, wup_ref, wdn_ref, y_ref, *, fc):
    half = GLA_V_W
    x1 = (x_ref[...] + _dot(mg_ref[...].astype(BF16), wo_ref[0:half, :])
          + _dot(md_ref[...].astype(BF16), wo_ref[half:, :]))
    r = lax.rsqrt(jnp.mean(x1 * x1, axis=-1, keepdims=True) + EPS)
    h2 = ((x1 * r) * g_ref[...]).astype(BF16)
    y_ref[...] = x1
    for f in range(D_FF // fc):
        u = jnp.maximum(_dot(h2, wup_ref[:, f * fc:(f + 1) * fc]), 0.0)
        y_ref[...] += _dot((u * u).astype(BF16), wdn_ref[f * fc:(f + 1) * fc, :])


def _out_ffn(x, mix_g, mix_d, wts, tm, fc=512):
    n = x.shape[0]
    assert n % tm == 0
    row = lambda w: pl.BlockSpec((tm, w), lambda i: (i, 0))
    consts = [wts['w_out'], wts['g_ffn'], wts['w_up'], wts['w_down']]
    const_specs = [_resident_spec(c.shape) for c in consts]
    return pl.pallas_call(
        functools.partial(_out_ffn_kernel, fc=fc),
        grid=(n // tm,),
        in_specs=[row(D_MODEL), row(GLA_V_W), row(DIFF_V_W)] + const_specs,
        out_specs=row(D_MODEL),
        out_shape=jax.ShapeDtypeStruct((n, D_MODEL), F32),
        compiler_params=_cparams(("parallel",)),
        name="out_ffn",
    )(x, mix_g, mix_d, *consts)


def _prep_weights(norm_mix_g, w_in, w_gk2, b_gk, gla_norm_g, q_norm_g, k_norm_g,
                  lambda_q1, lambda_k1, lambda_q2, lambda_k2, diff_norm_g, w_out,
                  norm_ffn_g, w_up, w_down):
    w = w_in[0]
    blk = np.arange(2 * LANES) // DH_DIFF
    ind = (np.arange(LANES)[:, None] // DK_GLA) == (np.arange(2 * DV_GLA)[None, :] // DV_GLA)
    return {
        'g_mix': norm_mix_g[0][None, :],
        'w_gla': w[:, :OFF_GLR].astype(BF16),
        'w_glr': jnp.pad(w[:, OFF_GLR:OFF_DIFF], ((0, 0), (0, LANES - GK_RANK))).astype(BF16),
        'w_gk2': jnp.pad(w_gk2[0], ((0, LANES - GK_RANK), (0, 0))).astype(BF16),
        'b_gk': b_gk[0][None, :],
        'w_diff': w[:, OFF_DIFF:].astype(BF16),
        'qn_g': jnp.tile(q_norm_g[0], 2 * H_DIFF)[None, :],
        'kn_g': jnp.tile(k_norm_g[0], 2 * H_DIFF)[None, :],
        'ones_blk': jnp.asarray(blk[:, None] == blk[None, :], BF16),
        'tri': jnp.asarray(np.tril(np.ones((GLA_C, GLA_C))), BF16),
        'ind': jnp.asarray(ind, BF16),
        'gla_g': gla_norm_g[0][None, :],
        'lq1': lambda_q1[0][None, :], 'lk1': lambda_k1[0][None, :],
        'lq2': lambda_q2[0][None, :], 'lk2': lambda_k2[0][None, :],
        'dn_g': diff_norm_g[0][None, :],
        'w_out': w_out[0].astype(BF16),
        'g_ffn': norm_ffn_g[0][None, :],
        'w_up': w_up[0].astype(BF16),
        'w_down': w_down[0].astype(BF16),
    }


def kernel(x_prompt, x_sample, cache_k, cache_v, page_table, state_gla, norm_mix_g, w_in, w_gk2, b_gk, gla_norm_g, q_norm_g, k_norm_g, lambda_q1, lambda_k1, lambda_q2, lambda_k2, diff_norm_g, w_out, norm_ffn_g, w_up, w_down):
    assert w_in.shape[0] == 1, "single-layer trunk"
    B, T, D = x_prompt.shape
    Bd, Td, _ = x_sample.shape
    assert Td == 1
    wts = _prep_weights(norm_mix_g, w_in, w_gk2, b_gk, gla_norm_g, q_norm_g, k_norm_g,
                        lambda_q1, lambda_k1, lambda_q2, lambda_k2, diff_norm_g, w_out,
                        norm_ffn_g, w_up, w_down)

    xp = x_prompt.reshape(B * T, D)
    xs = x_sample.reshape(Bd, D)
    qg, kg, vg, gg, la, qd_t, kdf_t, kdb, vdf_r, vdb_t = _in_proj(xp, wts, tm=TOK_TILE, seq=T)
    qg_s, kg_s, vg_s, gg_s, la_s, qd_s, kdf_s, _, vdf_s, _ = _in_proj(xs, wts, tm=Bd)

    mix_g, s_prompt = _gla_prompt(qg, kg, la, vg, gg, wts, B, T)
    mixg_s, s_sample = _gla_step(qg_s, kg_s, la_s, vg_s, gg_s, state_gla[0], wts)

    n_pool, page = cache_k.shape[1], cache_k.shape[2]
    cache_kt = jnp.transpose(cache_k[0], (0, 2, 3, 4, 1)).reshape(n_pool, DIFF_QK_W, page)
    cache_vr = cache_v[0].reshape(n_pool, page * H_DIFF, DV_DIFF)
    mix_d, mixd_s = _attention(qd_t, kdb, vdb_t, qd_s, kdf_s, vdf_s, cache_kt, cache_vr,
                               page_table, wts, B, T)

    y_prompt = _out_ffn(xp, mix_g, mix_d, wts, tm=TOK_TILE)
    y_sample = _out_ffn(xs, mixg_s, mixd_s, wts, tm=Bd)
    k_prompt = jnp.transpose(kdf_t.reshape(1, B, H_DIFF, 2, DH_DIFF, T), (0, 1, 5, 2, 3, 4))
    v_prompt = vdf_r.reshape(1, B, T, H_DIFF, DV_DIFF)

    return (y_prompt.reshape(B, T, D),
            y_sample.reshape(Bd, Td, D),
            k_prompt,
            v_prompt,
            s_prompt[None],
            kdf_s.reshape(1, Bd, Td, H_DIFF, 2, DH_DIFF),
            vdf_s.reshape(1, Bd, Td, H_DIFF, DV_DIFF),
            s_sample[None])
```

```python
import functools
import math

import jax
import jax.numpy as jnp
import numpy as np
from jax import lax
from jax.experimental import pallas as pl
from jax.experimental.pallas import tpu as pltpu

F32 = jnp.float32
BF16 = jnp.bfloat16

LANES = 128
SUBLANES = 8
VMEM_LIMIT = 56 * 1024 * 1024

D_MODEL = 1024
H_GLA = 4
DK_GLA = 64
DV_GLA = 128
GK_RANK = 16
GK_NORMALIZER = 16.0
H_DIFF = 4
DH_DIFF = 64
DV_DIFF = 128
D_FF = 4 * D_MODEL
EPS = 1e-6
LAMBDA_INIT = 0.8 - 0.6 * math.exp(-0.3 * 0)

GLA_QK_W = H_GLA * DK_GLA
GLA_V_W = H_GLA * DV_GLA
DIFF_QK_W = H_DIFF * 2 * DH_DIFF
DIFF_V_W = H_DIFF * DV_DIFF
OFF_GLR = 2 * GLA_QK_W + 2 * GLA_V_W
OFF_DIFF = OFF_GLR + GK_RANK

LOG2E = 1.4426950408889634
NEG_BIG = -1e30

GLA_C = 256
GLA_DIAG = 8
ATT_T = 512
ONES_ROWS = 16
TOK_TILE = 512


def _cparams(sem):
    return pltpu.CompilerParams(dimension_semantics=sem, vmem_limit_bytes=VMEM_LIMIT)


def _const_spec(shape):
    nd = len(shape)
    return pl.BlockSpec(shape, lambda *_: (0,) * nd)


def _resident_spec(shape):
    nd = len(shape)
    return pl.BlockSpec(shape, lambda *_: (0,) * nd, pipeline_mode=pl.Buffered(1))


def _dot(a, b):
    return jnp.dot(a, b, preferred_element_type=F32)


def _dot_nt(a, b):
    return lax.dot_general(a, b, (((1,), (1,)), ((), ())), preferred_element_type=F32)


def _lambda_full(lq1, lk1, lq2, lk2):
    a = jnp.sum(lq1 * lk1, axis=-1, keepdims=True)
    b = jnp.sum(lq2 * lk2, axis=-1, keepdims=True)
    return jnp.exp(a) - jnp.exp(b) + LAMBDA_INIT


def _group_rms(z, ones_blk, group):
    outs = []
    w = ones_blk.shape[0]
    for c in range(z.shape[1] // w):
        zc = z[:, c * w:(c + 1) * w]
        outs.append(_dot((zc * zc).astype(BF16), ones_blk))
    ss = jnp.concatenate(outs, axis=1)
    return lax.rsqrt(ss * (1.0 / group) + EPS)


def _in_proj_kernel(x_ref, g_ref, wgla_ref, wglr_ref, wgk2_ref, bgk_ref, wdiff_ref,
                    qng_ref, kng_ref, ones_ref,
                    qg_ref, kg_ref, vg_ref, gg_ref, la_ref,
                    qd_ref, kdf_ref, kdb_ref, vdf_ref, vdb_ref, *, cache_layout):
    x = x_ref[...]
    r = lax.rsqrt(jnp.mean(x * x, axis=-1, keepdims=True) + EPS)
    h = ((x * r) * g_ref[...]).astype(BF16)

    qg_ref[...] = _dot(h, wgla_ref[:, 0:GLA_QK_W]) * (DK_GLA ** -0.5)
    kg_ref[...] = _dot(h, wgla_ref[:, GLA_QK_W:2 * GLA_QK_W])
    vg_ref[...] = _dot(h, wgla_ref[:, 2 * GLA_QK_W:2 * GLA_QK_W + GLA_V_W])
    gg_ref[...] = _dot(h, wgla_ref[:, 2 * GLA_QK_W + GLA_V_W:])

    glr = _dot(h, wglr_ref[...])
    gl = _dot(glr.astype(BF16), wgk2_ref[...]) + bgk_ref[...]
    log_sig = jnp.minimum(gl, 0.0) - jnp.log1p(jnp.exp(-jnp.abs(gl)))
    la_ref[...] = log_sig / GK_NORMALIZER

    ones_blk = ones_ref[...]
    dq = _dot(h, wdiff_ref[:, 0:DIFF_QK_W])
    qn = (dq * _group_rms(dq, ones_blk, DH_DIFF)) * qng_ref[...]
    qs = qn * (DH_DIFF ** -0.5 * LOG2E)
    dk = _dot(h, wdiff_ref[:, DIFF_QK_W:2 * DIFF_QK_W])
    kn = (dk * _group_rms(dk, ones_blk, DH_DIFF)) * kng_ref[...]
    dv = _dot(h, wdiff_ref[:, 2 * DIFF_QK_W:])
    kdb_ref[...] = kn.astype(BF16)
    if cache_layout:
        qd_ref[0] = qs.T.astype(BF16)
        kdf_ref[0] = kn.T
        vdb_ref[0] = dv.T.astype(BF16)
        tm = dv.shape[0]
        for hh in range(H_DIFF):
            vdf_ref[pl.ds(hh, tm, stride=H_DIFF), :] = dv[:, hh * DV_DIFF:(hh + 1) * DV_DIFF]
    else:
        qd_ref[...] = qs.astype(BF16)
        kdf_ref[...] = kn
        vdf_ref[...] = dv
        vdb_ref[...] = dv.astype(BF16)


def _in_proj(x, wts, tm, seq=None):
    n = x.shape[0]
    assert n % tm == 0
    row = lambda w: pl.BlockSpec((tm, w), lambda i: (i, 0))
    out_w = [(GLA_QK_W, F32), (GLA_QK_W, F32), (GLA_V_W, F32), (GLA_V_W, F32), (GLA_QK_W, F32),
             (DIFF_QK_W, BF16), (DIFF_QK_W, F32), (DIFF_QK_W, BF16), (DIFF_V_W, F32), (DIFF_V_W, BF16)]
    out_specs = [row(w) for w, _ in out_w]
    out_shape = [jax.ShapeDtypeStruct((n, w), dt) for w, dt in out_w]
    if seq is not None:
        assert seq % tm == 0 and n % seq == 0
        per = seq // tm
        kt_spec = pl.BlockSpec((1, DIFF_QK_W, tm), lambda i: (i // per, 0, i % per))
        out_specs[5] = out_specs[6] = out_specs[9] = kt_spec
        out_shape[5] = jax.ShapeDtypeStruct((n // seq, DIFF_QK_W, seq), BF16)
        out_shape[6] = jax.ShapeDtypeStruct((n // seq, DIFF_QK_W, seq), F32)
        out_shape[9] = jax.ShapeDtypeStruct((n // seq, DIFF_V_W, seq), BF16)
        out_specs[8] = pl.BlockSpec((tm * H_DIFF, DV_DIFF), lambda i: (i, 0))
        out_shape[8] = jax.ShapeDtypeStruct((n * H_DIFF, DV_DIFF), F32)
    consts = [wts['g_mix'], wts['w_gla'], wts['w_glr'], wts['w_gk2'], wts['b_gk'], wts['w_diff'],
              wts['qn_g'], wts['kn_g'], wts['ones_blk']]
    return pl.pallas_call(
        functools.partial(_in_proj_kernel, cache_layout=seq is not None),
        grid=(n // tm,),
        in_specs=[row(D_MODEL)] + [_resident_spec(c.shape) for c in consts],
        out_specs=out_specs,
        out_shape=out_shape,
        compiler_params=_cparams(("parallel",)),
        name="in_proj",
    )(x, *consts)


def _bcast_row(x3, j):
    return jnp.broadcast_to(x3[:, j:j + 1, :], x3.shape)


def _gla_prompt_kernel(q_ref, k_ref, la_ref, v_ref, gate_ref, tri_ref, ind_ref, gn_ref,
                       o_ref, s_ref, st_ref):
    c = pl.program_id(2)
    C = q_ref.shape[0]
    neg_inf = -jnp.inf

    @pl.when(c == 0)
    def _():
        st_ref[...] = jnp.zeros_like(st_ref)

    la = la_ref[...]
    tri = tri_ref[...]
    t1 = la.astype(BF16)
    r1 = la - t1.astype(F32)
    t2 = r1.astype(BF16)
    t3 = (r1 - t2.astype(F32)).astype(BF16)
    b = (_dot(tri, t1) + _dot(tri, t2) + _dot(tri, t3)) * LOG2E

    q = q_ref[...]
    k = k_ref[...]
    v = v_ref[...]
    vb = v.astype(BF16)
    lane = lax.broadcasted_iota(jnp.int32, (C, LANES), 1)
    head0 = lane < DK_GLA

    G = C // GLA_DIAG
    b3 = b.reshape(G, GLA_DIAG, LANES)
    q3 = q.reshape(G, GLA_DIAG, LANES)
    k3 = k.reshape(G, GLA_DIAG, LANES)
    v3 = v.reshape(G, GLA_DIAG, 2 * DV_GLA)
    sub = lax.broadcasted_iota(jnp.int32, (G, GLA_DIAG, LANES), 1)
    ind = ind_ref[...]
    o = jnp.zeros((C, 2 * DV_GLA), F32)
    for j in range(GLA_DIAG):
        e = jnp.exp2(jnp.where(sub >= j, b3 - _bcast_row(b3, j), neg_inf))
        pj = (q3 * _bcast_row(k3, j) * e).reshape(C, LANES).astype(BF16)
        rj = _dot(pj, ind)
        o = o + rj * _bcast_row(v3, j).reshape(C, 2 * DV_GLA)

    ts_xor = (lax.broadcasted_iota(jnp.int32, (C, C), 0)
              ^ lax.broadcasted_iota(jnp.int32, (C, C), 1))
    tok = lax.broadcasted_iota(jnp.int32, (C, LANES), 0)
    a0 = jnp.zeros((C, C), F32)
    a1 = jnp.zeros((C, C), F32)
    m = GLA_DIAG
    while m < C:
        P = C // (2 * m)
        bmid = _bcast_row(b.reshape(P, 2 * m, LANES), m).reshape(C, LANES)
        late = (tok & m) != 0
        qt = q * jnp.exp2(jnp.where(late, b - bmid, neg_inf))
        kt = (k * jnp.exp2(jnp.where(late, neg_inf, bmid - b))).astype(BF16)
        split_here = (ts_xor // m) == 1
        x0 = _dot_nt(jnp.where(head0, qt, 0.0).astype(BF16), kt)
        x1 = _dot_nt(jnp.where(head0, 0.0, qt).astype(BF16), kt)
        a0 = jnp.where(split_here, x0, a0)
        a1 = jnp.where(split_here, x1, a1)
        m *= 2
    o_intra = jnp.concatenate([_dot(a0.astype(BF16), vb[:, :DV_GLA]),
                               _dot(a1.astype(BF16), vb[:, DV_GLA:])], axis=1)

    st = st_ref[...]
    stb = st.astype(BF16)
    qe = q * jnp.exp2(b)
    o_inter = jnp.concatenate(
        [_dot_nt(jnp.where(head0, qe, 0.0).astype(BF16), stb[:DV_GLA]),
         _dot_nt(jnp.where(head0, 0.0, qe).astype(BF16), stb[DV_GLA:])], axis=1)
    o = o + o_intra + o_inter

    b_last = b[C - 1:C, :]
    kd = (k * jnp.exp2(b_last - b)).astype(BF16)
    kv0 = _dot(vb[:, :DV_GLA].T, kd)
    kv1 = _dot(vb[:, DV_GLA:].T, kd)
    head0_s = lax.broadcasted_iota(jnp.int32, (DV_GLA, LANES), 1) < DK_GLA
    dec = jnp.exp2(b_last)
    st_new = jnp.concatenate([st[:DV_GLA] * dec + jnp.where(head0_s, kv0, 0.0),
                              st[DV_GLA:] * dec + jnp.where(head0_s, 0.0, kv1)], axis=0)
    st_ref[...] = st_new

    gate = gate_ref[...]
    gn = gn_ref[...]
    outs = []
    for hh in range(2):
        oh = o[:, hh * DV_GLA:(hh + 1) * DV_GLA]
        gh = gate[:, hh * DV_GLA:(hh + 1) * DV_GLA]
        r = lax.rsqrt(jnp.mean(oh * oh, axis=-1, keepdims=True) + EPS)
        outs.append(((oh * r) * gn) * (gh * (1.0 / (1.0 + jnp.exp(-gh)))))
    o_ref[...] = jnp.concatenate(outs, axis=1).astype(o_ref.dtype)

    @pl.when(c == pl.num_programs(2) - 1)
    def _():
        s0 = st_new[:DV_GLA].T
        s1 = st_new[DV_GLA:].T
        s_ref[0, 0] = s0[:DK_GLA]
        s_ref[0, 1] = s1[DK_GLA:]


def _gla_prompt(qg, kg, la, vg, gg, wts, batch, seq):
    C = GLA_C
    assert seq % C == 0
    nc = seq // C
    npair = H_GLA // 2
    qk_spec = pl.BlockSpec((C, LANES), lambda b, p, c: (b * nc + c, p))
    v_spec = pl.BlockSpec((C, 2 * DV_GLA), lambda b, p, c: (b * nc + c, p))
    return pl.pallas_call(
        _gla_prompt_kernel,
        grid=(batch, npair, nc),
        in_specs=[qk_spec, qk_spec, qk_spec, v_spec, v_spec,
                  _const_spec((C, C)), _const_spec((LANES, 2 * DV_GLA)), _const_spec((1, DV_GLA))],
        out_specs=[v_spec, pl.BlockSpec((1, 2, DK_GLA, DV_GLA), lambda b, p, c: (b, p, 0, 0))],
        out_shape=[jax.ShapeDtypeStruct((batch * seq, GLA_V_W), BF16),
                   jax.ShapeDtypeStruct((batch, H_GLA, DK_GLA, DV_GLA), F32)],
        scratch_shapes=[pltpu.VMEM((2 * DV_GLA, LANES), F32)],
        compiler_params=_cparams(("parallel", "parallel", "arbitrary")),
        name="gla_prompt",
    )(qg, kg, la, vg, gg, wts['tri'], wts['ind'], wts['gla_g'])


def _gla_step_kernel(q_ref, k_ref, la_ref, v_ref, gate_ref, s_ref, gn_ref, o_ref, so_ref):
    b = pl.program_id(0)
    nb = q_ref.shape[0]

    def column(ref):
        x = ref[...]
        pad = jnp.zeros((LANES - nb, x.shape[1]), F32)
        xt = jnp.concatenate([x, pad], axis=0).T
        lane = lax.broadcasted_iota(jnp.int32, xt.shape, 1)
        return jnp.sum(jnp.where(lane == b, xt, 0.0), axis=1, keepdims=True)

    qc = column(q_ref)
    kc = column(k_ref)
    ac = jnp.exp(column(la_ref))
    v = v_ref[0]
    gate = gate_ref[0]
    gn = gn_ref[...]
    outs = []
    for h in range(H_GLA):
        sl = slice(h * DK_GLA, (h + 1) * DK_GLA)
        vs = slice(h * DV_GLA, (h + 1) * DV_GLA)
        s_new = s_ref[0, h] * ac[sl] + kc[sl] * v[:, vs]
        so_ref[0, h] = s_new
        oh = jnp.sum(qc[sl] * s_new, axis=0, keepdims=True)
        r = lax.rsqrt(jnp.mean(oh * oh, axis=-1, keepdims=True) + EPS)
        gh = gate[:, vs]
        outs.append(((oh * r) * gn) * (gh * (1.0 / (1.0 + jnp.exp(-gh)))))
    o_ref[0] = jnp.concatenate(outs, axis=1)


def _gla_step(qg, kg, la, vg, gg, state, wts):
    nb = qg.shape[0]
    assert nb <= LANES
    full = _const_spec((nb, GLA_QK_W))
    row3 = pl.BlockSpec((1, 1, GLA_V_W), lambda b: (b, 0, 0))
    st_spec = pl.BlockSpec((1, H_GLA, DK_GLA, DV_GLA), lambda b: (b, 0, 0, 0))
    o, s = pl.pallas_call(
        _gla_step_kernel,
        grid=(nb,),
        in_specs=[full, full, full, row3, row3, st_spec, _const_spec((1, DV_GLA))],
        out_specs=[row3, st_spec],
        out_shape=[jax.ShapeDtypeStruct((nb, 1, GLA_V_W), F32),
                   jax.ShapeDtypeStruct(state.shape, F32)],
        compiler_params=_cparams(("parallel",)),
        name="gla_step",
    )(qg, kg, la, vg.reshape(nb, 1, GLA_V_W), gg.reshape(nb, 1, GLA_V_W), state, wts['gla_g'])
    return o.reshape(nb, GLA_V_W), s


def _diff_finish(o1, o2, lam, dn_g):
    o = o1 - lam * o2
    r = lax.rsqrt(jnp.mean(o * o, axis=-1, keepdims=True) + EPS)
    return ((o * r) * dn_g) * (1.0 - LAMBDA_INIT)


def _decode_init(q_ref, qbd_ref, m_ref, l_ref, acc_ref):
    R = 2 * H_DIFF
    rowi = lax.broadcasted_iota(jnp.int32, (R, DIFF_QK_W), 0)
    lanei = lax.broadcasted_iota(jnp.int32, (R, DIFF_QK_W), 1)
    qb = jnp.broadcast_to(q_ref[0], (R, DIFF_QK_W))
    qbd_ref[...] = jnp.where(lanei // DH_DIFF == rowi, qb, 0.0).astype(BF16)
    m_ref[...] = jnp.full_like(m_ref, NEG_BIG)
    l_ref[...] = jnp.zeros_like(l_ref)
    acc_ref[...] = jnp.zeros_like(acc_ref)


def _decode_pages(k_refs, v_refs, qbd_ref, m_ref, l_ref, acc_ref):
    qbd = qbd_ref[...]
    s = jnp.concatenate(
        [_dot(qbd, jnp.concatenate([k_refs[i][...], k_refs[i + 1][...]], axis=1).astype(BF16))
         for i in range(0, len(k_refs), 2)], axis=1)
    m_old = m_ref[...]
    m_new = jnp.maximum(m_old, jnp.max(s, axis=-1, keepdims=True))
    alpha = jnp.exp2(m_old - m_new)
    p = jnp.exp2(s - m_new)
    l_ref[...] = alpha * l_ref[...] + jnp.sum(p, axis=-1, keepdims=True)
    pb = p.astype(BF16)
    acc_ref[...] = alpha * acc_ref[...]
    for i in range(0, len(v_refs), 2):
        pi = pb[:, i * LANES:(i + 2) * LANES]
        for h in range(0, H_DIFF, 2):
            vh = jnp.concatenate(
                [jnp.concatenate([vr[pl.ds(hh, LANES, stride=H_DIFF), :] for hh in (h, h + 1)], axis=1)
                 for vr in (v_refs[i], v_refs[i + 1])], axis=0)
            acc_ref[:, h * DV_DIFF:(h + 2) * DV_DIFF] += _dot(pi, vh.astype(BF16))
    m_ref[...] = m_new


def _decode_finish(lam, kn_ref, vn_ref, dn_ref, o_ref, qbd_ref, m_ref, l_ref, acc_ref):
    R = 2 * H_DIFF
    rowi = lax.broadcasted_iota(jnp.int32, (R, DIFF_QK_W), 0)
    lanei = lax.broadcasted_iota(jnp.int32, (R, DIFF_QK_W), 1)
    m_old = m_ref[...]
    s_n = jnp.sum(qbd_ref[...].astype(F32) * kn_ref[0], axis=-1, keepdims=True)
    m_f = jnp.maximum(m_old, s_n)
    a_f = jnp.exp2(m_old - m_f)
    p_n = jnp.exp2(s_n - m_f)
    l_f = a_f * l_ref[...] + p_n
    acc_f = a_f * acc_ref[...] + p_n * vn_ref[0]
    coef = jnp.where((rowi & 1) == 0, 1.0, -lam)
    w = jnp.where(lanei // DV_DIFF == rowi // 2, (acc_f / l_f) * coef, 0.0)
    o = jnp.sum(w, axis=0, keepdims=True)
    dn = dn_ref[...]
    outs = []
    for h in range(H_DIFF):
        oh = o[:, h * DV_DIFF:(h + 1) * DV_DIFF]
        outs.append(_diff_finish(oh, jnp.zeros_like(oh), lam, dn))
    o_ref[0] = jnp.concatenate(outs, axis=1)


def _page_copies(pt_ref, ck_hbm, cv_hbm, kbuf, vbuf, sem, step, slot, pps):
    copies = []
    for k in range(pps):
        pg = pt_ref[step * pps + k]
        copies.append(pltpu.make_async_copy(ck_hbm.at[pg], kbuf.at[slot, k], sem.at[0, slot]))
        copies.append(pltpu.make_async_copy(cv_hbm.at[pg], vbuf.at[slot, k], sem.at[1, slot]))
    return copies


def _attention_kernel(pt_ref, qt_ref, k_ref, vt_ref, lq1, lk1, lq2, lk2, dn_ref, dnc_ref,
                      dq_ref, dkn_ref, dvn_ref, ck_hbm, cv_hbm, o_ref, do_ref,
                      m_ref, acc_ref, sa_ref, sb_ref, qbd_ref, dm_ref, dl_ref, dacc_ref,
                      kbuf, vbuf, sem, *, n_pages_step, n_chunks):
    qi = pl.program_id(2)
    lam = _lambda_full(lq1[...], lk1[...], lq2[...], lk2[...])
    n_steps = pl.num_programs(0) * pl.num_programs(1) * pl.num_programs(2)
    step = (pl.program_id(0) * pl.num_programs(1) + pl.program_id(1)) * pl.num_programs(2) + qi
    chunk = lax.rem(step, n_chunks)
    slot = lax.rem(step, 2)

    @pl.when(step == 0)
    def _():
        for cp in _page_copies(pt_ref, ck_hbm, cv_hbm, kbuf, vbuf, sem, 0, 0, n_pages_step):
            cp.start()

    @pl.when(step + 1 < n_steps)
    def _():
        for cp in _page_copies(pt_ref, ck_hbm, cv_hbm, kbuf, vbuf, sem, step + 1, 1 - slot,
                               n_pages_step):
            cp.start()

    @pl.when(chunk == 0)
    def _():
        _decode_init(dq_ref, qbd_ref, dm_ref, dl_ref, dacc_ref)

    for cp in _page_copies(pt_ref, ck_hbm, cv_hbm, kbuf, vbuf, sem, step, slot, n_pages_step):
        cp.wait()
    _decode_pages([kbuf.at[slot, k] for k in range(n_pages_step)],
                  [vbuf.at[slot, k] for k in range(n_pages_step)],
                  qbd_ref, dm_ref, dl_ref, dacc_ref)

    T = qt_ref.shape[2]
    qt = qt_ref[0]
    dim = lax.broadcasted_iota(jnp.int32, (LANES, T), 0)
    zero = jnp.zeros_like(qt)
    qq = jnp.concatenate([jnp.where(dim < DH_DIFF, qt, zero),
                          jnp.where(dim < DH_DIFF, zero, qt)], axis=1)
    ones = jnp.ones((ONES_ROWS, T), BF16)
    key_i = lax.broadcasted_iota(jnp.int32, (T, 2 * T), 0)
    qry_i = lax.broadcasted_iota(jnp.int32, (T, 2 * T), 1) & (T - 1)
    causal = key_i <= qry_i

    def scores(tile, s_ref):
        s_ref[...] = _dot(k_ref[pl.ds(pl.multiple_of(tile * T, T), T), :], qq)

    def update(tile, s_ref, masked):
        v1 = jnp.concatenate([vt_ref[0, :, pl.ds(pl.multiple_of(tile * T, T), T)], ones], axis=0)
        s = s_ref[...]
        if masked:
            s = jnp.where(causal, s, -jnp.inf)
        m_old = m_ref[...]
        m_new = jnp.maximum(m_old, jnp.max(s, axis=0, keepdims=True))
        alpha = jnp.exp2(m_old - m_new)
        p = jnp.exp2(s - m_new)
        acc_ref[...] = alpha * acc_ref[...] + _dot(v1, p.astype(BF16))
        m_ref[...] = m_new

    m_ref[...] = jnp.full_like(m_ref, -jnp.inf)
    acc_ref[...] = jnp.zeros_like(acc_ref)
    scores(0, sa_ref)

    def body(j, carry):
        t = 2 * j
        scores(t + 1, sb_ref)
        update(t, sa_ref, False)
        scores(t + 2, sa_ref)
        update(t + 1, sb_ref, False)
        return carry

    lax.fori_loop(0, lax.shift_right_logical(qi, 1), body, 0)

    @pl.when((qi & 1) == 0)
    def _():
        update(qi, sa_ref, True)

    @pl.when((qi & 1) == 1)
    def _():
        scores(qi, sb_ref)
        update(qi - 1, sa_ref, False)
        update(qi, sb_ref, True)

    acc = acc_ref[...]
    on = acc[:DV_DIFF] / acc[DV_DIFF:DV_DIFF + 1]
    o = on[:, :T] - lam * on[:, T:]
    r = lax.rsqrt(jnp.mean(o * o, axis=0, keepdims=True) + EPS)
    o = ((o * r) * dnc_ref[...]) * (1.0 - LAMBDA_INIT)
    o_ref[...] = o.T.astype(o_ref.dtype)

    @pl.when(chunk == n_chunks - 1)
    def _():
        _decode_finish(lam, dkn_ref, dvn_ref, dn_ref, do_ref, qbd_ref, dm_ref, dl_ref, dacc_ref)


def _attention(qdt, kd, vdt, qd_s, kd_s, vd_s, cache_kt, cache_vr, page_table, wts, batch, seq):
    T = ATT_T
    assert seq % T == 0 and (T & (T - 1)) == 0
    nq = seq // T
    nb, n_pages = page_table.shape
    assert cache_kt.shape[2] == LANES
    n_steps = batch * H_DIFF * nq
    assert (nb * n_pages) % n_steps == 0
    pps = (nb * n_pages) // n_steps
    assert n_pages % pps == 0 and pps % 2 == 0
    n_chunks = n_pages // pps

    def step_of(b, h, i):
        return (b * H_DIFF + h) * nq + i

    o_spec = pl.BlockSpec((T, LANES), lambda b, h, i, pt: (b * nq + i, h))
    qt_spec = pl.BlockSpec((1, LANES, T), lambda b, h, i, pt: (b, h, i))
    k_spec = pl.BlockSpec((seq, LANES), lambda b, h, i, pt: (b, h))
    vt_spec = pl.BlockSpec((1, LANES, seq), lambda b, h, i, pt: (b, h, 0))
    lam_spec = pl.BlockSpec((1, DH_DIFF), lambda b, h, i, pt: (0, 0))
    row3 = pl.BlockSpec((1, 1, DIFF_QK_W), lambda b, h, i, pt: (step_of(b, h, i) // n_chunks, 0, 0))

    grid_spec = pltpu.PrefetchScalarGridSpec(
        num_scalar_prefetch=1,
        grid=(batch, H_DIFF, nq),
        in_specs=[qt_spec, k_spec, vt_spec, lam_spec, lam_spec, lam_spec, lam_spec,
                  pl.BlockSpec((1, DV_DIFF), lambda b, h, i, pt: (0, 0)),
                  pl.BlockSpec((DV_DIFF, 1), lambda b, h, i, pt: (0, 0)), row3, row3, row3,
                  pl.BlockSpec(memory_space=pl.ANY), pl.BlockSpec(memory_space=pl.ANY)],
        out_specs=[o_spec, row3],
        scratch_shapes=[pltpu.VMEM((1, 2 * T), F32), pltpu.VMEM((DV_DIFF + ONES_ROWS, 2 * T), F32),
                        pltpu.VMEM((T, 2 * T), F32), pltpu.VMEM((T, 2 * T), F32),
                        pltpu.VMEM((2 * H_DIFF, DIFF_QK_W), BF16),
                        pltpu.VMEM((2 * H_DIFF, 1), F32), pltpu.VMEM((2 * H_DIFF, 1), F32),
                        pltpu.VMEM((2 * H_DIFF, DIFF_V_W), F32),
                        pltpu.VMEM((2, pps, DIFF_QK_W, LANES), F32),
                        pltpu.VMEM((2, pps, DIFF_QK_W, LANES), F32),
                        pltpu.SemaphoreType.DMA((2, 2))],
    )
    o, o_s = pl.pallas_call(
        functools.partial(_attention_kernel, n_pages_step=pps, n_chunks=n_chunks),
        grid_spec=grid_spec,
        out_shape=[jax.ShapeDtypeStruct((batch * seq, DIFF_V_W), BF16),
                   jax.ShapeDtypeStruct((nb, 1, DIFF_V_W), F32)],
        compiler_params=_cparams(("arbitrary", "arbitrary", "arbitrary")),
        name="attention",
    )(page_table.reshape(-1), qdt, kd, vdt,
      wts['lq1'], wts['lk1'], wts['lq2'], wts['lk2'], wts['dn_g'], wts['dn_g'].reshape(DV_DIFF, 1),
      qd_s.astype(F32).reshape(nb, 1, DIFF_QK_W), kd_s.reshape(nb, 1, DIFF_QK_W),
      vd_s.reshape(nb, 1, DIFF_V_W), cache_kt, cache_vr)
    return o, o_s.reshape(nb, DIFF_V_W)


def _out_ffn_kernel(x_ref, mg_ref, md_ref, wo_ref, g_ref, wup_ref, wdn_ref, y_ref, *, fc):
    half = GLA_V_W
    x1 = (x_ref[...] + _dot(mg_ref[...].astype(BF16), wo_ref[0:half, :])
          + _dot(md_ref[...].astype(BF16), wo_ref[half:, :]))
    r = lax.rsqrt(jnp.mean(x1 * x1, axis=-1, keepdims=True) + EPS)
    h2 = ((x1 * r) * g_ref[...]).astype(BF16)
    y_ref[...] = x1
    for f in range(D_FF // fc):
        u = jnp.maximum(_dot(h2, wup_ref[:, f * fc:(f + 1) * fc]), 0.0)
        y_ref[...] += _dot((u * u).astype(BF16), wdn_ref[f * fc:(f + 1) * fc, :])


def _out_ffn(x, mix_g, mix_d, wts, tm, fc=512):
    n = x.shape[0]
    assert n % tm == 0
    row = lambda w: pl.BlockSpec((tm, w), lambda i: (i, 0))
    consts = [wts['w_out'], wts['g_ffn'], wts['w_up'], wts['w_down']]
    const_specs = [_resident_spec(c.shape) for c in consts]
    return pl.pallas_call(
        functools.partial(_out_ffn_kernel, fc=fc),
        grid=(n // tm,),
        in_specs=[row(D_MODEL), row(GLA_V_W), row(DIFF_V_W)] + const_specs,
        out_specs=row(D_MODEL),
        out_shape=jax.ShapeDtypeStruct((n, D_MODEL), F32),
        compiler_params=_cparams(("parallel",)),
        name="out_ffn",
    )(x, mix_g, mix_d, *consts)


def _prep_weights(norm_mix_g, w_in, w_gk2, b_gk, gla_norm_g, q_norm_g, k_norm_g,
                  lambda_q1, lambda_k1, lambda_q2, lambda_k2, diff_norm_g, w_out,
                  norm_ffn_g, w_up, w_down):
    w = w_in[0]
    blk = np.arange(2 * LANES) // DH_DIFF
    ind = (np.arange(LANES)[:, None] // DK_GLA) == (np.arange(2 * DV_GLA)[None, :] // DV_GLA)
    return {
        'g_mix': norm_mix_g[0][None, :],
        'w_gla': w[:, :OFF_GLR].astype(BF16),
        'w_glr': jnp.pad(w[:, OFF_GLR:OFF_DIFF], ((0, 0), (0, LANES - GK_RANK))).astype(BF16),
        'w_gk2': jnp.pad(w_gk2[0], ((0, LANES - GK_RANK), (0, 0))).astype(BF16),
        'b_gk': b_gk[0][None, :],
        'w_diff': w[:, OFF_DIFF:].astype(BF16),
        'qn_g': jnp.tile(q_norm_g[0], 2 * H_DIFF)[None, :],
        'kn_g': jnp.tile(k_norm_g[0], 2 * H_DIFF)[None, :],
        'ones_blk': jnp.asarray(blk[:, None] == blk[None, :], BF16),
        'tri': jnp.asarray(np.tril(np.ones((GLA_C, GLA_C))), BF16),
        'ind': jnp.asarray(ind, BF16),
        'gla_g': gla_norm_g[0][None, :],
        'lq1': lambda_q1[0][None, :], 'lk1': lambda_k1[0][None, :],
        'lq2': lambda_q2[0][None, :], 'lk2': lambda_k2[0][None, :],
        'dn_g': diff_norm_g[0][None, :],
        'w_out': w_out[0].astype(BF16),
        'g_ffn': norm_ffn_g[0][None, :],
        'w_up': w_up[0].astype(BF16),
        'w_down': w_down[0].astype(BF16),
    }


def kernel(x_prompt, x_sample, cache_k, cache_v, page_table, state_gla, norm_mix_g, w_in, w_gk2, b_gk, gla_norm_g, q_norm_g, k_norm_g, lambda_q1, lambda_k1, lambda_q2, lambda_k2, diff_norm_g, w_out, norm_ffn_g, w_up, w_down):
    assert w_in.shape[0] == 1, "single-layer trunk"
    B, T, D = x_prompt.shape
    Bd, Td, _ = x_sample.shape
    assert Td == 1
    wts = _prep_weights(norm_mix_g, w_in, w_gk2, b_gk, gla_norm_g, q_norm_g, k_norm_g,
                        lambda_q1, lambda_k1, lambda_q2, lambda_k2, diff_norm_g, w_out,
                        norm_ffn_g, w_up, w_down)

    xp = x_prompt.reshape(B * T, D)
    xs = x_sample.reshape(Bd, D)
    qg, kg, vg, gg, la, qd_t, kdf_t, kdb, vdf_r, vdb_t = _in_proj(xp, wts, tm=TOK_TILE, seq=T)
    qg_s, kg_s, vg_s, gg_s, la_s, qd_s, kdf_s, _, vdf_s, _ = _in_proj(xs, wts, tm=Bd)

    mix_g, s_prompt = _gla_prompt(qg, kg, la, vg, gg, wts, B, T)
    mixg_s, s_sample = _gla_step(qg_s, kg_s, la_s, vg_s, gg_s, state_gla[0], wts)

    n_pool, page = cache_k.shape[1], cache_k.shape[2]
    cache_kt = jnp.transpose(cache_k[0], (0, 2, 3, 4, 1)).reshape(n_pool, DIFF_QK_W, page)
    cache_vr = cache_v[0].reshape(n_pool, page * H_DIFF, DV_DIFF)
    mix_d, mixd_s = _attention(qd_t, kdb, vdb_t, qd_s, kdf_s, vdf_s, cache_kt, cache_vr,
                               page_table, wts, B, T)

    y_prompt = _out_ffn(xp, mix_g, mix_d, wts, tm=TOK_TILE)
    y_sample = _out_ffn(xs, mixg_s, mixd_s, wts, tm=Bd)
    k_prompt = jnp.transpose(kdf_t.reshape(1, B, H_DIFF, 2, DH_DIFF, T), (0, 1, 5, 2, 3, 4))
    v_prompt = vdf_r.reshape(1, B, T, H_DIFF, DV_DIFF)

    return (y_prompt.reshape(B, T, D),
            y_sample.reshape(Bd, Td, D),
            k_prompt,
            v_prompt,
            s_prompt[None],
            kdf_s.reshape(1, Bd, Td, H_DIFF, 2, DH_DIFF),
            vdf_s.reshape(1, Bd, Td, H_DIFF, DV_DIFF),
            s_sample[None])
```
